```python
import math
import jax, jax.numpy as jnp
from jax import lax
import numpy as np

D_MODEL = 1024
BATCH = 4
SEQ = 4096
DEPTH = 4
DEC_BATCH = 16
DEC_SEQ = 4096
PAST_LEN = 128

DILATED_PAIRS = ((128, 1), (512, 4), (2048, 16))
N_GROUPS_A = len(DILATED_PAIRS)
HEADS_PER_GROUP_A = 4
HEAD_DIM_A = 128
WIDTH_A = HEADS_PER_GROUP_A * HEAD_DIM_A
QKV_A = N_GROUPS_A * WIDTH_A
BAND_BLOCK = 64
N_HEADS_B = 4
HEAD_DIM_B = 64
V_DIM_B = 2 * HEAD_DIM_B
WIDTH_B = N_HEADS_B * V_DIM_B
Q_BLOCK = 128
MIX_WIDTH = WIDTH_A + WIDTH_B
IN_COLS = 3 * QKV_A + 3 * WIDTH_B
N_EXPERTS = 16
D_FF = 2816
CAPACITY_FACTOR = 2
ROPE_THETA = 10000.0
LN_EPS = 1e-5
NEG = -1e30
ALPHA = (2 * DEPTH) ** 0.25
BETA = (8 * DEPTH) ** -0.25

kernel_name = "hybrid_dilated_diff_ec_encoder"


def layer_norm(x, g, b):
    xf = x.astype(jnp.float32)
    mu = jnp.mean(xf, axis=-1, keepdims=True)
    var = jnp.mean(jnp.square(xf - mu), axis=-1, keepdims=True)
    return ((xf - mu) * lax.rsqrt(var + LN_EPS) * g.astype(jnp.float32) + b.astype(jnp.float32)).astype(x.dtype)


def rope(x, pos):
    d = x.shape[-1]
    inv = ROPE_THETA ** (-jnp.arange(0, d, 2, dtype=jnp.float32) / d)
    ang = pos.astype(jnp.float32)[:, None] * inv[None, :]
    bshape = (ang.shape[0],) + (1,) * (x.ndim - 3) + (d // 2,)
    cos = jnp.cos(ang).reshape(bshape)
    sin = jnp.sin(ang).reshape(bshape)
    xf = x.astype(jnp.float32)
    x1, x2 = xf[..., : d // 2], xf[..., d // 2:]
    return jnp.concatenate([x1 * cos - x2 * sin, x1 * sin + x2 * cos], axis=-1).astype(x.dtype)


def dilated_window_attention(q, k, v, window, dilation):
    B, S, H, dh = q.shape
    r = dilation
    radius = window // (2 * dilation)
    L = S // r
    BB = BAND_BLOCK
    nb = -(-L // BB)
    Lp = nb * BB

    def by_residue(t):
        return t.reshape(B, L, r, H, dh).transpose(0, 2, 1, 3, 4)

    qs, ks, vs = by_residue(q), by_residue(k), by_residue(v)
    qs = jnp.pad(qs, ((0, 0), (0, 0), (0, Lp - L), (0, 0), (0, 0)))
    kpad = ((0, 0), (0, 0), (BB, Lp - L + BB), (0, 0), (0, 0))
    ks = jnp.pad(ks, kpad)
    vs = jnp.pad(vs, kpad)
    qb = qs.reshape(B, r, nb, BB, H, dh)
    idx = jnp.arange(nb)[:, None] * BB + jnp.arange(3 * BB)[None, :]
    kb = ks[:, :, idx]
    vb = vs[:, :, idx]
    qpos = jnp.arange(nb)[:, None] * BB + jnp.arange(BB)[None, :]
    kpos = idx - BB
    mask = ((jnp.abs(qpos[:, :, None] - kpos[:, None, :]) <= radius)
            & (kpos >= 0)[:, None, :] & (kpos < L)[:, None, :])
    s = jnp.einsum('brnqhd,brnkhd->brnhqk', qb, kb).astype(jnp.float32) * (dh ** -0.5)
    s = jnp.where(mask[None, None, :, None], s, NEG)
    lse = jax.nn.logsumexp(s, axis=-1)
    p = jnp.exp(s - lse[..., None])
    o = jnp.einsum('brnhqk,brnkhd->brnqhd', p, vb.astype(jnp.float32))
    o = o.reshape(B, r, Lp, H, dh)[:, :, :L].transpose(0, 2, 1, 3, 4).reshape(B, S, H, dh)
    lse = lse.transpose(0, 1, 2, 4, 3).reshape(B, r, Lp, H)[:, :, :L].transpose(0, 2, 1, 3).reshape(B, S, H)
    return o, lse


def diff_attention(q, k, v, lam, subln_g, lambda_init):
    B, S, H, _, d = q.shape
    nq = S // Q_BLOCK
    qblocks = q.reshape(B, nq, Q_BLOCK, H, 2, d).transpose(1, 0, 2, 3, 4, 5)
    vf = v.astype(jnp.float32)
    scale = d ** -0.5

    def block(qblk):
        s = jnp.einsum('bqhtd,bkhtd->bhtqk', qblk, k).astype(jnp.float32) * scale
        p = jax.nn.softmax(s, axis=-1)
        a = p[:, :, 0] - lam * p[:, :, 1]
        return jnp.einsum('bhqk,bkhe->bqhe', a, vf)

    o = lax.map(block, qblocks)
    o = o.transpose(1, 0, 2, 3, 4).reshape(B, S, H, V_DIM_B)
    o = o * lax.rsqrt(jnp.mean(jnp.square(o), axis=-1, keepdims=True) + LN_EPS) * subln_g.astype(jnp.float32)
    return o * (1.0 - lambda_init)


def token_mixer(x, w_in, w_out, lq1, lk1, lq2, lk2, subln_g, lambda_init):
    B, S, _ = x.shape
    pos = jnp.arange(S)
    proj = jnp.einsum('bsd,dc->bsc', x, w_in)
    splits = [QKV_A, 2 * QKV_A, 3 * QKV_A, 3 * QKV_A + WIDTH_B, 3 * QKV_A + 2 * WIDTH_B]
    qa, ka, va, qb, kb, vb = jnp.split(proj, splits, axis=-1)
    gshape = (B, S, N_GROUPS_A, HEADS_PER_GROUP_A, HEAD_DIM_A)
    qa = rope(qa.reshape(gshape), pos)
    ka = rope(ka.reshape(gshape), pos)
    va = va.reshape(gshape)
    outs, lses = [], []
    for g, (window, dilation) in enumerate(DILATED_PAIRS):
        o, l = dilated_window_attention(qa[:, :, g], ka[:, :, g], va[:, :, g], window, dilation)
        outs.append(o)
        lses.append(l)
    wts = jax.nn.softmax(jnp.stack(lses, axis=0), axis=0)
    out_a = jnp.sum(wts[..., None] * jnp.stack(outs, axis=0), axis=0).reshape(B, S, WIDTH_A)
    qb = rope(qb.reshape(B, S, N_HEADS_B, 2, HEAD_DIM_B), pos)
    kb = rope(kb.reshape(B, S, N_HEADS_B, 2, HEAD_DIM_B), pos)
    vb = vb.reshape(B, S, N_HEADS_B, V_DIM_B)
    lam = (jnp.exp(jnp.sum(lq1.astype(jnp.float32) * lk1.astype(jnp.float32)))
           - jnp.exp(jnp.sum(lq2.astype(jnp.float32) * lk2.astype(jnp.float32))) + lambda_init)
    out_b = diff_attention(qb, kb, vb, lam, subln_g, lambda_init).reshape(B, S, WIDTH_B)
    mixed = jnp.concatenate([out_a, out_b], axis=-1).astype(x.dtype)
    return jnp.einsum('bsc,cd->bsd', mixed, w_out)


def expert_choice_ffn(x, w_router, w_gate, w_up, w_down):
    B, S, D = x.shape
    n = B * S
    cap = CAPACITY_FACTOR * n // N_EXPERTS
    xt = x.reshape(n, D)
    aff = jax.nn.softmax(jnp.einsum('nd,de->ne', xt, w_router).astype(jnp.float32), axis=-1)
    gates, idx = lax.top_k(aff.T, cap)
    xe = xt[idx]

    def one_expert(args):
        xi, gi, w1, w3, w2 = args
        h = jax.nn.silu(xi @ w1) * (xi @ w3)
        return (h @ w2) * gi[:, None].astype(xi.dtype)

    ye = lax.map(one_expert, (xe, gates, w_gate, w_up, w_down))
    y = jnp.zeros_like(xt).at[idx.reshape(-1)].add(ye.reshape(-1, D))
    return y.reshape(B, S, D)


def trunk(x, w_in, w_out, lambda_q1, lambda_k1, lambda_q2, lambda_k2, subln_g,
          ln1_g, ln1_b, w_router, w_gate, w_up, w_down, ln2_g, ln2_b):
    for l in range(DEPTH):
        lambda_init = 0.8 - 0.6 * math.exp(-0.3 * l)
        m = token_mixer(x, w_in[l], w_out[l], lambda_q1[l], lambda_k1[l], lambda_q2[l], lambda_k2[l],
                        subln_g[l], lambda_init)
        x = layer_norm(ALPHA * x + m, ln1_g[l], ln1_b[l])
        f = expert_choice_ffn(x, w_router[l], w_gate[l], w_up[l], w_down[l])
        x = layer_norm(ALPHA * x + f, ln2_g[l], ln2_b[l])
    return x


def setup_inputs(seed: int = 0) -> dict:
    key = jax.random.key(seed)
    ks = jax.random.split(key, 20)
    f32 = jnp.float32
    nrm = lambda k, shape, s: jax.random.normal(k, shape, f32) * s
    return {
        "x_prompt": nrm(ks[0], (BATCH, SEQ, D_MODEL), 1.0),
        "x_sample": nrm(ks[1], (DEC_BATCH, DEC_SEQ, D_MODEL), 1.0),
        "w_in": nrm(ks[2], (DEPTH, D_MODEL, IN_COLS), D_MODEL ** -0.5),
        "w_out": nrm(ks[3], (DEPTH, MIX_WIDTH, D_MODEL), BETA * MIX_WIDTH ** -0.5),
        "lambda_q1": nrm(ks[4], (DEPTH, HEAD_DIM_B), 0.1),
        "lambda_k1": nrm(ks[5], (DEPTH, HEAD_DIM_B), 0.1),
        "lambda_q2": nrm(ks[6], (DEPTH, HEAD_DIM_B), 0.1),
        "lambda_k2": nrm(ks[7], (DEPTH, HEAD_DIM_B), 0.1),
        "subln_g": 1.0 + nrm(ks[8], (DEPTH, V_DIM_B), 0.02),
        "ln1_g": 1.0 + nrm(ks[9], (DEPTH, D_MODEL), 0.02),
        "ln1_b": nrm(ks[10], (DEPTH, D_MODEL), 0.02),
        "w_router": nrm(ks[11], (DEPTH, D_MODEL, N_EXPERTS), D_MODEL ** -0.5),
        "w_gate": nrm(ks[12], (DEPTH, N_EXPERTS, D_MODEL, D_FF), D_MODEL ** -0.5),
        "w_up": nrm(ks[13], (DEPTH, N_EXPERTS, D_MODEL, D_FF), D_MODEL ** -0.5),
        "w_down": nrm(ks[14], (DEPTH, N_EXPERTS, D_FF, D_MODEL), BETA * D_FF ** -0.5),
        "ln2_g": 1.0 + nrm(ks[15], (DEPTH, D_MODEL), 0.02),
        "ln2_b": nrm(ks[16], (DEPTH, D_MODEL), 0.02),
    }


def reference(x_prompt, x_sample, w_in, w_out, lambda_q1, lambda_k1, lambda_q2, lambda_k2, subln_g,
              ln1_g, ln1_b, w_router, w_gate, w_up, w_down, ln2_g, ln2_b):
    y_prompt = trunk(x_prompt, w_in, w_out, lambda_q1, lambda_k1, lambda_q2, lambda_k2, subln_g,
                     ln1_g, ln1_b, w_router, w_gate, w_up, w_down, ln2_g, ln2_b)
    y_sample = trunk(x_sample, w_in, w_out, lambda_q1, lambda_k1, lambda_q2, lambda_k2, subln_g,
                     ln1_g, ln1_b, w_router, w_gate, w_up, w_down, ln2_g, ln2_b)
    return (y_prompt, y_sample)
```

```python
import functools
import math

import jax
import jax.numpy as jnp
from jax import lax
from jax.experimental import pallas as pl
from jax.experimental.pallas import tpu as pltpu

D_MODEL = 1024
DEPTH = 4
DILATIONS = (1, 4, 16)
RADIUS = 64
HEADS_A = 4
HEAD_DIM = 128
WIDTH_A = HEADS_A * HEAD_DIM
QKV_A = len(DILATIONS) * WIDTH_A
HEADS_B = 4
SUB_DIM_B = 64
WIDTH_B = HEADS_B * HEAD_DIM
IN_COLS = 3 * QKV_A + 3 * WIDTH_B
N_EXPERTS = 16
D_FF = 2816
CAPACITY_FACTOR = 2
ROPE_THETA = 10000.0
LN_EPS = 1e-5
NEG = -1e30
ALPHA = (2 * DEPTH) ** 0.25
LOG2E = 1.4426950408889634
LN2 = 0.6931471805599453

QA_BLK, KA_BLK, VA_BLK = 0, QKV_A // HEAD_DIM, 2 * QKV_A // HEAD_DIM
QB_BLK = 3 * QKV_A // HEAD_DIM
KB_BLK = QB_BLK + WIDTH_B // HEAD_DIM
VB_BLK = KB_BLK + WIDTH_B // HEAD_DIM
N_BLKS = IN_COLS // HEAD_DIM

VMEM_LIMIT = 56 * 1024 * 1024
BF16 = jnp.bfloat16
F32 = jnp.float32


def _cparams(*sem):
    return pltpu.CompilerParams(dimension_semantics=sem, vmem_limit_bytes=VMEM_LIMIT)


IN_TM = 512
IN_CHUNK = 512


def _in_proj_kernel(x_ref, w_ref, tab_ref, out_ref):
    xb = x_ref[...].astype(BF16)
    blocks_per_chunk = IN_CHUNK // HEAD_DIM
    for ch in range(IN_COLS // IN_CHUNK):
        c0 = ch * IN_CHUNK
        acc = jnp.dot(xb, w_ref[:, c0:c0 + IN_CHUNK], preferred_element_type=F32)
        for j in range(blocks_per_chunk):
            blk_id = ch * blocks_per_chunk + j
            blk = acc[:, j * HEAD_DIM:(j + 1) * HEAD_DIM]
            if blk_id < KA_BLK:
                t = 0
            elif blk_id < VA_BLK:
                t = 2
            elif blk_id < QB_BLK:
                t = None
            elif blk_id < KB_BLK:
                t = 4
            elif blk_id < VB_BLK:
                t = 6
            else:
                t = None
            if t is not None:
                blk = blk * tab_ref[t] + pltpu.roll(blk, HEAD_DIM // 2, 1) * tab_ref[t + 1]
            out_ref[:, c0 + j * HEAD_DIM:c0 + (j + 1) * HEAD_DIM] = blk.astype(BF16)


def _in_proj(x2d, w_bf, tables, seq):
    n = x2d.shape[0]
    tm = min(IN_TM, seq)
    tiles_per_seq = seq // tm
    return pl.pallas_call(
        _in_proj_kernel,
        grid=(n // tm,),
        in_specs=[
            pl.BlockSpec((tm, D_MODEL), lambda i: (i, 0)),
            pl.BlockSpec((D_MODEL, IN_COLS), lambda i: (0, 0), pipeline_mode=pl.Buffered(1)),
            pl.BlockSpec((8, tm, HEAD_DIM), lambda i: (0, i % tiles_per_seq, 0)),
        ],
        out_specs=pl.BlockSpec((tm, IN_COLS), lambda i: (i, 0)),
        out_shape=jax.ShapeDtypeStruct((n, IN_COLS), BF16),
        compiler_params=_cparams("parallel"),
        name="in_proj_rope",
    )(x2d, w_bf, tables)


def _rope_tables(seq):
    pos = jnp.arange(seq, dtype=F32)[:, None]

    def cs(d):
        inv = ROPE_THETA ** (-jnp.arange(0, d, 2, dtype=F32) / d)
        ang = pos * inv[None, :]
        return jnp.cos(ang), jnp.sin(ang)

    ca, sa = cs(HEAD_DIM)
    cos_a = jnp.concatenate([ca, ca], axis=1)
    sin_a = jnp.concatenate([-sa, sa], axis=1)
    cb, sb = cs(SUB_DIM_B)
    cos_b = jnp.concatenate([cb, cb, cb, cb], axis=1)
    sin_b = jnp.concatenate([-sb, -sb, sb, sb], axis=1)
    qa = HEAD_DIM ** -0.5 * LOG2E
    qb = SUB_DIM_B ** -0.5 * LOG2E
    return jnp.stack([cos_a * qa, sin_a * qa, cos_a, sin_a, cos_b * qb, sin_b * qb, cos_b, sin_b], axis=0)


def _prep_w_in(w_in):
    half = SUB_DIM_B // 2
    perm = []
    for quarter in range(4):
        t, hi = quarter % 2, quarter // 2
        perm.extend(t * SUB_DIM_B + hi * half + i for i in range(half))
    perm = jnp.asarray(perm, dtype=jnp.int32)
    cols = jnp.arange(IN_COLS, dtype=jnp.int32)
    start, stop = QB_BLK * HEAD_DIM, VB_BLK * HEAD_DIM
    inside = (cols >= start) & (cols < stop)
    rel = cols - start
    permuted = start + (rel // HEAD_DIM) * HEAD_DIM + perm[rel % HEAD_DIM]
    src = jnp.where(inside, permuted, cols)
    return jnp.take(w_in, src, axis=1).astype(BF16)


A_QB = 128
A_WIN = A_QB + 2 * RADIUS


def _dilated_kernel(q_ref, k_ref, v_ref, o_ref, lse_ref, *, sub_len, heads):
    nq = sub_len // A_QB
    row = lax.broadcasted_iota(jnp.int32, (A_QB, A_WIN), 0)
    col = lax.broadcasted_iota(jnp.int32, (A_QB, A_WIN), 1)
    rel0 = col - row

    for h in range(heads):
        lanes = slice(h * HEAD_DIM, (h + 1) * HEAD_DIM)

        def body(qi, carry, lanes=lanes):
            q0 = pl.multiple_of(qi * A_QB, A_QB)
            ws = pl.multiple_of(jnp.clip(q0 - RADIUS, 0, sub_len - A_WIN), RADIUS)
            qb = q_ref[0, pl.ds(q0, A_QB), lanes]
            kw = k_ref[0, pl.ds(ws, A_WIN), lanes]
            vw = v_ref[0, pl.ds(ws, A_WIN), lanes]
            s = lax.dot_general(qb, kw, (((1,), (1,)), ((), ())), preferred_element_type=F32)
            rel = rel0 - (q0 - ws)
            s = jnp.where(jnp.abs(rel) <= RADIUS, s, NEG)
            m = jnp.max(s, axis=1, keepdims=True)
            p = jnp.exp2(s - m)
            l = jnp.sum(p, axis=1, keepdims=True)
            o = jnp.dot(p.astype(BF16), vw, preferred_element_type=F32)
            o_ref[0, pl.ds(q0, A_QB), lanes] = o / l
            lse = (m + jnp.log2(l)) * LN2
            lse_ref[0, pl.ds(q0, A_QB), lanes] = jnp.broadcast_to(lse, (A_QB, HEAD_DIM))
            return carry

        lax.fori_loop(0, nq, body, 0)


def _dilated_group(proj, batch, seq, g):
    r = DILATIONS[g]
    sub_len = seq // r
    assert sub_len % A_QB == 0 and sub_len >= A_WIN
    heads = 1 if r == 1 else HEADS_A
    width = heads * HEAD_DIM
    steps = HEADS_A // heads
    per_res = IN_COLS // width
    view = proj.reshape(batch, sub_len, r * IN_COLS)

    def in_map(sec_blk):
        base = (sec_blk * HEAD_DIM + g * WIDTH_A) // width
        return lambda b, c, hh: (b, 0, c * per_res + base + hh)

    out_map = lambda b, c, hh: (b, 0, c * steps + hh)
    blk = (1, sub_len, width)
    o, lse = pl.pallas_call(
        functools.partial(_dilated_kernel, sub_len=sub_len, heads=heads),
        grid=(batch, r, steps),
        in_specs=[pl.BlockSpec(blk, in_map(QA_BLK)), pl.BlockSpec(blk, in_map(KA_BLK)),
                  pl.BlockSpec(blk, in_map(VA_BLK))],
        out_specs=[pl.BlockSpec(blk, out_map), pl.BlockSpec(blk, out_map)],
        out_shape=[jax.ShapeDtypeStruct((batch, sub_len, r * WIDTH_A), F32)] * 2,
        compiler_params=_cparams("parallel", "parallel", "parallel"),
        name=f"dilated_attn_r{r}",
    )(view, view, view)
    return o.reshape(batch * seq, WIDTH_A), lse.reshape(batch * seq, WIDTH_A)


B_TQ = 256
B_TK = 512


def _diff_kernel(scal_ref, q_ref, k_ref, v_ref, g_ref, o_ref, *, seq):
    lam = scal_ref[0]
    out_scale = scal_ref[1]
    q = q_ref[0]
    lane = lax.broadcasted_iota(jnp.int32, q.shape, 1)
    first = ((lane // (SUB_DIM_B // 2)) % 2) == 0
    zero = jnp.zeros_like(q)
    qs = (jnp.where(first, q, zero), jnp.where(first, zero, q))
    tq = q.shape[0]

    def body(j, carry):
        k0 = pl.multiple_of(j * B_TK, B_TK)
        kc = k_ref[0, pl.ds(k0, B_TK), :]
        vc = v_ref[0, pl.ds(k0, B_TK), :]
        new = []
        for t in range(2):
            m_prev, l_prev, acc_prev = carry[t]
            s = lax.dot_general(qs[t], kc, (((1,), (1,)), ((), ())), preferred_element_type=F32)
            m_new = jnp.maximum(m_prev, jnp.max(s, axis=1, keepdims=True))
            a = jnp.exp2(m_prev - m_new)
            p = jnp.exp2(s - m_new)
            l_new = a * l_prev + jnp.sum(p, axis=1, keepdims=True)
            acc_new = a * acc_prev + jnp.dot(p.astype(BF16), vc, preferred_element_type=F32)
            new.append((m_new, l_new, acc_new))
        return tuple(new)

    init = tuple((jnp.full((tq, 1), -jnp.inf, F32), jnp.zeros((tq, 1), F32), jnp.zeros((tq, HEAD_DIM), F32))
                 for _ in range(2))
    (_, l0, acc0), (_, l1, acc1) = lax.fori_loop(0, seq // B_TK, body, init)
    o = acc0 / l0 - lam * (acc1 / l1)
    o = o * lax.rsqrt(jnp.mean(o * o, axis=1, keepdims=True) + LN_EPS) * g_ref[...]
    o_ref[0] = (o * out_scale).astype(BF16)


def _diff_attention(proj, scalars, subln_g, batch, seq):
    view = proj.reshape(batch, seq, IN_COLS)
    tq = min(B_TQ, seq)
    out = pl.pallas_call(
        functools.partial(_diff_kernel, seq=seq),
        grid=(batch, HEADS_B, seq // tq),
        in_specs=[
            pl.BlockSpec(memory_space=pltpu.SMEM),
            pl.BlockSpec((1, tq, HEAD_DIM), lambda b, h, i: (b, i, QB_BLK + h)),
            pl.BlockSpec((1, seq, HEAD_DIM), lambda b, h, i: (b, 0, KB_BLK + h)),
            pl.BlockSpec((1, seq, HEAD_DIM), lambda b, h, i: (b, 0, VB_BLK + h)),
            pl.BlockSpec((1, HEAD_DIM), lambda b, h, i: (0, 0)),
        ],
        out_specs=pl.BlockSpec((1, tq, HEAD_DIM), lambda b, h, i: (b, i, h)),
        out_shape=jax.ShapeDtypeStruct((batch, seq, WIDTH_B), BF16),
        compiler_params=_cparams("parallel", "parallel", "arbitrary"),
        name="diff_attn",
    )(scalars, view, view, view, subln_g.reshape(1, HEAD_DIM).astype(F32))
    return out.reshape(batch * seq, WIDTH_B)


OUT_TM = 256


def _layer_norm(y, g, b):
    mu = jnp.mean(y, axis=1, keepdims=True)
    d = y - mu
    var = jnp.mean(d * d, axis=1, keepdims=True)
    return d * lax.rsqrt(var + LN_EPS) * g + b


def _out_proj_kernel(o0, o1, o2, l0, l1, l2, mb_ref, x_ref, w_ref, g_ref, b_ref, wr_ref,
                     x1_ref, x1b_ref, aff_ref):
    la, lb, lc = l0[...], l1[...], l2[...]
    mx = jnp.maximum(jnp.maximum(la, lb), lc)
    ea, eb, ec = jnp.exp(la - mx), jnp.exp(lb - mx), jnp.exp(lc - mx)
    mix_a = (ea * o0[...] + eb * o1[...] + ec * o2[...]) / (ea + eb + ec)
    m = jnp.dot(mix_a.astype(BF16), w_ref[:WIDTH_A, :], preferred_element_type=F32)
    m = m + jnp.dot(mb_ref[...], w_ref[WIDTH_A:, :], preferred_element_type=F32)
    x1 = _layer_norm(ALPHA * x_ref[...] + m, g_ref[...], b_ref[...])
    x1_ref[...] = x1
    x1b_ref[...] = x1.astype(BF16)
    logits = jnp.dot(x1, wr_ref[...], preferred_element_type=F32, precision=lax.Precision.HIGHEST)
    e = jnp.exp(logits - jnp.max(logits, axis=1, keepdims=True))
    aff_ref[...] = e / jnp.sum(e, axis=1, keepdims=True)


def _out_proj(o_groups, lse_groups, mix_b, x2d, w_out_bf, ln_g, ln_b, w_router):
    n = x2d.shape[0]
    tm = OUT_TM
    row_a = pl.BlockSpec((tm, WIDTH_A), lambda i: (i, 0))
    row_d = pl.BlockSpec((tm, D_MODEL), lambda i: (i, 0))
    const = lambda shape: pl.BlockSpec(shape, lambda i: (0, 0))
    return pl.pallas_call(
        _out_proj_kernel,
        grid=(n // tm,),
        in_specs=[row_a] * 7 + [row_d, const((D_MODEL, D_MODEL)), const((1, D_MODEL)), const((1, D_MODEL)),
                                const((D_MODEL, N_EXPERTS))],
        out_specs=[row_d, row_d, pl.BlockSpec((tm, N_EXPERTS), lambda i: (i, 0))],
        out_shape=[jax.ShapeDtypeStruct((n, D_MODEL), F32), jax.ShapeDtypeStruct((n, D_MODEL), BF16),
                   jax.ShapeDtypeStruct((n, N_EXPERTS), F32)],
        compiler_params=_cparams("parallel"),
        name="out_proj_ln_router",
    )(*o_groups, *lse_groups, mix_b, x2d, w_out_bf, ln_g.reshape(1, D_MODEL), ln_b.reshape(1, D_MODEL), w_router)


FFN_TM = 512
FFN_CHUNK = 256


def _ffn_kernel(x_ref, gate_ref, w1_ref, w3_ref, w2_ref, o_ref):
    x = x_ref[0]
    acc = jnp.zeros((x.shape[0], D_MODEL), F32)
    for c in range(D_FF // FFN_CHUNK):
        cols = slice(c * FFN_CHUNK, (c + 1) * FFN_CHUNK)
        g = jnp.dot(x, w1_ref[0, :, cols], preferred_element_type=F32)
        u = jnp.dot(x, w3_ref[0, :, cols], preferred_element_type=F32)
        h = (g * jax.nn.sigmoid(g) * u).astype(BF16)
        acc = acc + jnp.dot(h, w2_ref[0, cols, :], preferred_element_type=F32)
    o_ref[0] = acc * gate_ref[0]


def _expert_ffn(xe, gates, w1, w3, w2):
    e, rows, _ = xe.shape
    tm = FFN_TM
    return pl.pallas_call(
        _ffn_kernel,
        grid=(e, rows // tm),
        in_specs=[
            pl.BlockSpec((1, tm, D_MODEL), lambda ei, i: (ei, i, 0)),
            pl.BlockSpec((1, tm, 1), lambda ei, i: (ei, i, 0)),
            pl.BlockSpec((1, D_MODEL, D_FF), lambda ei, i: (ei, 0, 0)),
            pl.BlockSpec((1, D_MODEL, D_FF), lambda ei, i: (ei, 0, 0)),
            pl.BlockSpec((1, D_FF, D_MODEL), lambda ei, i: (ei, 0, 0)),
        ],
        out_specs=pl.BlockSpec((1, tm, D_MODEL), lambda ei, i: (ei, i, 0)),
        out_shape=jax.ShapeDtypeStruct((e, rows, D_MODEL), F32),
        compiler_params=_cparams("arbitrary", "arbitrary"),
        name="expert_ffn",
    )(xe, gates, w1, w3, w2)


LN_TM = 512


def _residual_ln_kernel(x_ref, f_ref, g_ref, b_ref, o_ref):
    o_ref[...] = _layer_norm(ALPHA * x_ref[...] + f_ref[...], g_ref[...], b_ref[...])


def _residual_ln(x2d, f2d, ln_g, ln_b):
    n = x2d.shape[0]
    row = pl.BlockSpec((LN_TM, D_MODEL), lambda i: (i, 0))
    const = pl.BlockSpec((1, D_MODEL), lambda i: (0, 0))
    return pl.pallas_call(
        _residual_ln_kernel,
        grid=(n // LN_TM,),
        in_specs=[row, row, const, const],
        out_specs=row,
        out_shape=jax.ShapeDtypeStruct((n, D_MODEL), F32),
        compiler_params=_cparams("parallel"),
        name="residual_ln",
    )(x2d, f2d, ln_g.reshape(1, D_MODEL), ln_b.reshape(1, D_MODEL))


def _trunk(x2d, group_rows, batch, seq, w_in, w_out, lambda_q1, lambda_k1, lambda_q2, lambda_k2, subln_g,
           ln1_g, ln1_b, w_router, w_gate, w_up, w_down, ln2_g, ln2_b):
    tables = _rope_tables(seq)
    depth = w_in.shape[0]
    for l in range(depth):
        lambda_init = 0.8 - 0.6 * math.exp(-0.3 * l)
        lam = (jnp.exp(jnp.sum(lambda_q1[l] * lambda_k1[l])) - jnp.exp(jnp.sum(lambda_q2[l] * lambda_k2[l]))
               + lambda_init)
        scalars = jnp.stack([lam, jnp.asarray(1.0 - lambda_init, F32)]).astype(F32)

        proj = _in_proj(x2d, _prep_w_in(w_in[l]), tables, seq)
        o_groups, lse_groups = [], []
        for g in range(len(DILATIONS)):
            o, lse = _dilated_group(proj, batch, seq, g)
            o_groups.append(o)
            lse_groups.append(lse)
        mix_b = _diff_attention(proj, scalars, subln_g[l], batch, seq)
        x1, x1b, aff = _out_proj(o_groups, lse_groups, mix_b, x2d, w_out[l].astype(BF16), ln1_g[l], ln1_b[l],
                                 w_router[l])

        xe_parts, gate_parts, idx_parts = [], [], []
        start = 0
        for rows in group_rows:
            cap = CAPACITY_FACTOR * rows // N_EXPERTS
            gates, idx = lax.top_k(aff[start:start + rows].T, cap)
            xe_parts.append(x1b[start:start + rows][idx])
            gate_parts.append(gates)
            idx_parts.append(idx + start)
            start += rows
        xe = jnp.concatenate(xe_parts, axis=1)
        gates = jnp.concatenate(gate_parts, axis=1)
        idx = jnp.concatenate(idx_parts, axis=1)
        ye = _expert_ffn(xe, gates[..., None], w_gate[l].astype(BF16), w_up[l].astype(BF16),
                         w_down[l].astype(BF16))
        f = jnp.zeros_like(x1).at[idx.reshape(-1)].add(ye.reshape(-1, D_MODEL))
        x2d = _residual_ln(x1, f, ln2_g[l], ln2_b[l])
    return x2d


def kernel(x_prompt, x_sample, w_in, w_out, lambda_q1, lambda_k1, lambda_q2, lambda_k2, subln_g, ln1_g, ln1_b,
           w_router, w_gate, w_up, w_down, ln2_g, ln2_b):
    bp, seq, d = x_prompt.shape
    bs = x_sample.shape[0]
    assert x_sample.shape[1] == seq
    x2d = jnp.concatenate([x_prompt.reshape(bp * seq, d), x_sample.reshape(bs * seq, d)], axis=0)
    y = _trunk(x2d, (bp * seq, bs * seq), bp + bs, seq, w_in, w_out, lambda_q1, lambda_k1, lambda_q2, lambda_k2,
               subln_g, ln1_g, ln1_b, w_router, w_gate, w_up, w_down, ln2_g, ln2_b)
    return y[:bp * seq].reshape(bp, seq, d), y[bp * seq:].reshape(bs, seq, d)
```

```python
import functools
import math

import jax
import jax.numpy as jnp
from jax import lax
from jax.experimental import pallas as pl
from jax.experimental.pallas import tpu as pltpu

D_MODEL = 1024
DEPTH = 4
DILATIONS = (1, 4, 16)
RADIUS = 64
HEADS_A = 4
HEAD_DIM = 128
WIDTH_A = HEADS_A * HEAD_DIM
QKV_A = len(DILATIONS) * WIDTH_A
HEADS_B = 4
SUB_DIM_B = 64
WIDTH_B = HEADS_B * HEAD_DIM
IN_COLS = 3 * QKV_A + 3 * WIDTH_B
N_EXPERTS = 16
D_FF = 2816
CAPACITY_FACTOR = 2
ROPE_THETA = 10000.0
LN_EPS = 1e-5
NEG = -1e30
ALPHA = (2 * DEPTH) ** 0.25
LOG2E = 1.4426950408889634
LN2 = 0.6931471805599453

QA_BLK, KA_BLK, VA_BLK = 0, QKV_A // HEAD_DIM, 2 * QKV_A // HEAD_DIM
QB_BLK = 3 * QKV_A // HEAD_DIM
KB_BLK = QB_BLK + WIDTH_B // HEAD_DIM
VB_BLK = KB_BLK + WIDTH_B // HEAD_DIM
N_BLKS = IN_COLS // HEAD_DIM

VMEM_LIMIT = 56 * 1024 * 1024
BF16 = jnp.bfloat16
F32 = jnp.float32


def _cparams(*sem):
    return pltpu.CompilerParams(dimension_semantics=sem, vmem_limit_bytes=VMEM_LIMIT)


IN_TM = 512
IN_CHUNK = 512


def _in_proj_kernel(x_ref, w_ref, tab_ref, a0_ref, a1_ref, a2_ref, b_ref, scr_ref):
    xb = x_ref[...].astype(BF16)
    tm = xb.shape[0]
    group_refs = (a0_ref, a1_ref, a2_ref)
    n_groups = len(DILATIONS)
    for ch in range(IN_COLS // IN_CHUNK):
        c0 = ch * IN_CHUNK
        acc = jnp.dot(xb, w_ref[:, c0:c0 + IN_CHUNK], preferred_element_type=F32)
        if ch < 3 * n_groups:
            sec, g = divmod(ch, n_groups)
            tab = (0, 2, None)[sec]
        else:
            sec, g = ch - 3 * n_groups, None
            tab = (4, 6, None)[sec]
        blocks = []
        for j in range(IN_CHUNK // HEAD_DIM):
            blk = acc[:, j * HEAD_DIM:(j + 1) * HEAD_DIM]
            if tab is not None:
                blk = blk * tab_ref[tab] + pltpu.roll(blk, HEAD_DIM // 2, 1) * tab_ref[tab + 1]
            blocks.append(blk)
        if g is None or g == 0:
            dst = b_ref if g is None else a0_ref
            for j, blk in enumerate(blocks):
                lo = sec * WIDTH_A + j * HEAD_DIM
                dst[:, lo:lo + HEAD_DIM] = blk.astype(BF16)
        else:
            r = DILATIONS[g]
            for j, blk in enumerate(blocks):
                scr_ref[j] = blk
            for c in range(r):
                for j in range(len(blocks)):
                    rows = scr_ref[j, pl.ds(c, tm // r, stride=r), :]
                    lo = c * 3 * WIDTH_A + sec * WIDTH_A + j * HEAD_DIM
                    group_refs[g][0, :, lo:lo + HEAD_DIM] = rows.astype(BF16)


def _in_proj(x2d, w_bf, tables, batch, seq):
    n = x2d.shape[0]
    tm = min(IN_TM, seq)
    tiles_per_seq = seq // tm
    r1, r2 = DILATIONS[1], DILATIONS[2]
    qkv = 3 * WIDTH_A
    seq_map = lambda i: (i // tiles_per_seq, i % tiles_per_seq, 0)
    return pl.pallas_call(
        _in_proj_kernel,
        grid=(n // tm,),
        in_specs=[
            pl.BlockSpec((tm, D_MODEL), lambda i: (i, 0)),
            pl.BlockSpec((D_MODEL, IN_COLS), lambda i: (0, 0), pipeline_mode=pl.Buffered(1)),
            pl.BlockSpec((8, tm, HEAD_DIM), lambda i: (0, i % tiles_per_seq, 0)),
        ],
        out_specs=[
            pl.BlockSpec((tm, qkv), lambda i: (i, 0)),
            pl.BlockSpec((1, tm // r1, r1 * qkv), seq_map),
            pl.BlockSpec((1, tm // r2, r2 * qkv), seq_map),
            pl.BlockSpec((tm, qkv), lambda i: (i, 0)),
        ],
        out_shape=[
            jax.ShapeDtypeStruct((n, qkv), BF16),
            jax.ShapeDtypeStruct((batch, seq // r1, r1 * qkv), BF16),
            jax.ShapeDtypeStruct((batch, seq // r2, r2 * qkv), BF16),
            jax.ShapeDtypeStruct((n, qkv), BF16),
        ],
        scratch_shapes=[pltpu.VMEM((IN_CHUNK // HEAD_DIM, tm, HEAD_DIM), F32)],
        compiler_params=_cparams("parallel"),
        name="in_proj_rope",
    )(x2d, w_bf, tables)


def _rope_tables(seq):
    pos = jnp.arange(seq, dtype=F32)[:, None]

    def cs(d):
        inv = ROPE_THETA ** (-jnp.arange(0, d, 2, dtype=F32) / d)
        ang = pos * inv[None, :]
        return jnp.cos(ang), jnp.sin(ang)

    ca, sa = cs(HEAD_DIM)
    cos_a = jnp.concatenate([ca, ca], axis=1)
    sin_a = jnp.concatenate([-sa, sa], axis=1)
    cb, sb = cs(SUB_DIM_B)
    cos_b = jnp.concatenate([cb, cb, cb, cb], axis=1)
    sin_b = jnp.concatenate([-sb, -sb, sb, sb], axis=1)
    qa = HEAD_DIM ** -0.5 * LOG2E
    qb = SUB_DIM_B ** -0.5 * LOG2E
    return jnp.stack([cos_a * qa, sin_a * qa, cos_a, sin_a, cos_b * qb, sin_b * qb, cos_b, sin_b], axis=0)


def _prep_w_in(w_in):
    half = SUB_DIM_B // 2
    perm = []
    for quarter in range(4):
        t, hi = quarter % 2, quarter // 2
        perm.extend(t * SUB_DIM_B + hi * half + i for i in range(half))
    perm = jnp.asarray(perm, dtype=jnp.int32)
    cols = jnp.arange(IN_COLS, dtype=jnp.int32)
    start, stop = QB_BLK * HEAD_DIM, VB_BLK * HEAD_DIM
    inside = (cols >= start) & (cols < stop)
    rel = cols - start
    permuted = start + (rel // HEAD_DIM) * HEAD_DIM + perm[rel % HEAD_DIM]
    src = jnp.where(inside, permuted, cols)
    return jnp.take(w_in, src, axis=1).astype(BF16)


A_QB = 128
A_WIN = A_QB + 2 * RADIUS
A_UNROLL = 8


def _dilated_kernel(q_ref, k_ref, v_ref, o_ref, lse_ref, *, sub_len, heads):
    nq = sub_len // A_QB
    row = lax.broadcasted_iota(jnp.int32, (A_QB, A_WIN), 0)
    col = lax.broadcasted_iota(jnp.int32, (A_QB, A_WIN), 1)
    rel0 = col - row

    for h in range(heads):
        lanes = slice(h * HEAD_DIM, (h + 1) * HEAD_DIM)

        def body(qi, carry, lanes=lanes):
            q0 = pl.multiple_of(qi * A_QB, A_QB)
            ws = pl.multiple_of(jnp.clip(q0 - RADIUS, 0, sub_len - A_WIN), RADIUS)
            qb = q_ref[0, pl.ds(q0, A_QB), lanes]
            kw = k_ref[0, pl.ds(ws, A_WIN), lanes]
            vw = v_ref[0, pl.ds(ws, A_WIN), lanes]
            s = lax.dot_general(qb, kw, (((1,), (1,)), ((), ())), preferred_element_type=F32)
            rel = rel0 - (q0 - ws)
            s = jnp.where(jnp.abs(rel) <= RADIUS, s, NEG)
            m = jnp.max(s, axis=1, keepdims=True)
            p = jnp.exp2(s - m)
            l = jnp.sum(p, axis=1, keepdims=True)
            o = jnp.dot(p.astype(BF16), vw, preferred_element_type=F32)
            o_ref[0, pl.ds(q0, A_QB), lanes] = o / l
            lse = (m + jnp.log2(l)) * LN2
            lse_ref[0, pl.ds(q0, A_QB), lanes] = jnp.broadcast_to(lse, (A_QB, HEAD_DIM))
            return carry

        lax.fori_loop(0, nq, body, 0, unroll=min(nq, A_UNROLL))


def _dilated_group(qkv, batch, seq, g):
    r = DILATIONS[g]
    sub_len = seq // r
    assert sub_len % A_QB == 0 and sub_len >= A_WIN
    heads = 1 if r == 1 else HEADS_A
    width = heads * HEAD_DIM
    steps = HEADS_A // heads
    per_sec = WIDTH_A // width
    view = qkv.reshape(batch, sub_len, r * 3 * WIDTH_A)

    def in_map(sec):
        return lambda b, c, hh: (b, 0, (c * 3 + sec) * per_sec + hh)

    out_map = lambda b, c, hh: (b, 0, c * steps + hh)
    blk = (1, sub_len, width)
    return pl.pallas_call(
        functools.partial(_dilated_kernel, sub_len=sub_len, heads=heads),
        grid=(batch, r, steps),
        in_specs=[pl.BlockSpec(blk, in_map(0)), pl.BlockSpec(blk, in_map(1)), pl.BlockSpec(blk, in_map(2))],
        out_specs=[pl.BlockSpec(blk, out_map), pl.BlockSpec(blk, out_map)],
        out_shape=[jax.ShapeDtypeStruct((batch, sub_len, r * WIDTH_A), F32)] * 2,
        compiler_params=_cparams("parallel", "parallel", "parallel"),
        name=f"dilated_attn_r{r}",
    )(view, view, view)


B_TQ = 512
B_TK = 512
B_UNROLL = 4


def _diff_kernel(scal_ref, q_ref, k_ref, vt_ref, g_ref, o_ref, s_scr, *, seq):
    lam = scal_ref[0]
    out_scale = scal_ref[1]
    q = q_ref[0]
    tq = q.shape[0]
    lane = lax.broadcasted_iota(jnp.int32, q.shape, 1)
    first = ((lane // (SUB_DIM_B // 2)) % 2) == 0
    zero = jnp.zeros_like(q)
    nk = seq // B_TK
    outs = []
    for t in range(2):
        qt = jnp.where(first, q, zero) if t == 0 else jnp.where(first, zero, q)

        def pass1(j, m, qt=qt):
            k0 = pl.multiple_of(j * B_TK, B_TK)
            s = lax.dot_general(k_ref[0, pl.ds(k0, B_TK), :], qt, (((1,), (1,)), ((), ())),
                                preferred_element_type=F32)
            s_scr[pl.ds(k0, B_TK), :] = s
            return jnp.maximum(m, jnp.max(s, axis=0, keepdims=True))

        m = lax.fori_loop(0, nk, pass1, jnp.full((1, tq), -jnp.inf, F32), unroll=B_UNROLL)

        def pass2(j, carry, m=m):
            l, acc = carry
            k0 = pl.multiple_of(j * B_TK, B_TK)
            p = jnp.exp2(s_scr[pl.ds(k0, B_TK), :] - m)
            l = l + jnp.sum(p, axis=0, keepdims=True)
            acc = acc + jnp.dot(vt_ref[0, 0, :, pl.ds(k0, B_TK)], p.astype(BF16), preferred_element_type=F32)
            return l, acc

        l, acc = lax.fori_loop(0, nk, pass2, (jnp.zeros((1, tq), F32), jnp.zeros((HEAD_DIM, tq), F32)),
                               unroll=B_UNROLL)
        outs.append(acc / l)
    o = outs[0] - lam * outs[1]
    o = o * lax.rsqrt(jnp.mean(o * o, axis=0, keepdims=True) + LN_EPS) * g_ref[...]
    o_ref[0] = (o * out_scale).T.astype(BF16)


def _diff_attention(qkv_b, scalars, subln_g, batch, seq):
    view = qkv_b.reshape(batch, seq, 3 * WIDTH_B)
    v_t = view[:, :, 2 * WIDTH_B:].reshape(batch, seq, HEADS_B, HEAD_DIM).transpose(0, 2, 3, 1)
    tq = min(B_TQ, seq)
    out = pl.pallas_call(
        functools.partial(_diff_kernel, seq=seq),
        grid=(batch, HEADS_B, seq // tq),
        in_specs=[
            pl.BlockSpec(memory_space=pltpu.SMEM),
            pl.BlockSpec((1, tq, HEAD_DIM), lambda b, h, i: (b, i, h)),
            pl.BlockSpec((1, seq, HEAD_DIM), lambda b, h, i: (b, 0, HEADS_B + h)),
            pl.BlockSpec((1, 1, HEAD_DIM, seq), lambda b, h, i: (b, h, 0, 0)),
            pl.BlockSpec((HEAD_DIM, 1), lambda b, h, i: (0, 0)),
        ],
        out_specs=pl.BlockSpec((1, tq, HEAD_DIM), lambda b, h, i: (b, i, h)),
        out_shape=jax.ShapeDtypeStruct((batch, seq, WIDTH_B), BF16),
        scratch_shapes=[pltpu.VMEM((seq, tq), F32)],
        compiler_params=_cparams("parallel", "parallel", "arbitrary"),
        name="diff_attn",
    )(scalars, view, view, v_t, subln_g.reshape(HEAD_DIM, 1).astype(F32))
    return out.reshape(batch * seq, WIDTH_B)


OUT_TM = 256


def _layer_norm(y, g, b):
    mu = jnp.mean(y, axis=1, keepdims=True)
    d = y - mu
    var = jnp.mean(d * d, axis=1, keepdims=True)
    return d * lax.rsqrt(var + LN_EPS) * g + b


def _token_order(ref, r, head, scr):
    rows = scr.shape[0]
    for c in range(r):
        lo = c * WIDTH_A + head * HEAD_DIM
        scr[pl.ds(c, rows // r, stride=r), :] = ref[0, :, lo:lo + HEAD_DIM]
    return scr[...]


def _out_proj_kernel(o0, o1, o2, l0, l1, l2, mb_ref, x_ref, w_ref, g_ref, b_ref, wr_ref,
                     x1_ref, x1b_ref, aff_ref, so1, so2, sl1, sl2):
    r1, r2 = DILATIONS[1], DILATIONS[2]
    heads = []
    for h in range(HEADS_A):
        lanes = slice(h * HEAD_DIM, (h + 1) * HEAD_DIM)
        la, lb, lc = l0[:, lanes], _token_order(l1, r1, h, sl1), _token_order(l2, r2, h, sl2)
        mx = jnp.maximum(jnp.maximum(la, lb), lc)
        ea, eb, ec = jnp.exp(la - mx), jnp.exp(lb - mx), jnp.exp(lc - mx)
        num = ea * o0[:, lanes] + eb * _token_order(o1, r1, h, so1) + ec * _token_order(o2, r2, h, so2)
        heads.append((num / (ea + eb + ec)).astype(BF16))
    mix_a = jnp.concatenate(heads, axis=1)
    m = jnp.dot(mix_a, w_ref[:WIDTH_A, :], preferred_element_type=F32)
    m = m + jnp.dot(mb_ref[...], w_ref[WIDTH_A:, :], preferred_element_type=F32)
    x1 = _layer_norm(ALPHA * x_ref[...] + m, g_ref[...], b_ref[...])
    x1_ref[...] = x1
    x1b_ref[...] = x1.astype(BF16)
    logits = jnp.dot(x1, wr_ref[...], preferred_element_type=F32, precision=lax.Precision.HIGHEST)
    e = jnp.exp(logits - jnp.max(logits, axis=1, keepdims=True))
    aff_ref[...] = e / jnp.sum(e, axis=1, keepdims=True)


def _out_proj(o_groups, lse_groups, mix_b, x2d, w_out_bf, ln_g, ln_b, w_router, seq):
    n = x2d.shape[0]
    tm = OUT_TM
    tiles_per_seq = seq // tm
    r1, r2 = DILATIONS[1], DILATIONS[2]
    row_a = pl.BlockSpec((tm, WIDTH_A), lambda i: (i, 0))
    row_d = pl.BlockSpec((tm, D_MODEL), lambda i: (i, 0))
    seq_map = lambda i: (i // tiles_per_seq, i % tiles_per_seq, 0)
    res1 = pl.BlockSpec((1, tm // r1, r1 * WIDTH_A), seq_map)
    res2 = pl.BlockSpec((1, tm // r2, r2 * WIDTH_A), seq_map)
    const = lambda shape: pl.BlockSpec(shape, lambda i: (0, 0))
    return pl.pallas_call(
        _out_proj_kernel,
        grid=(n // tm,),
        in_specs=[row_a, res1, res2, row_a, res1, res2, row_a, row_d, const((D_MODEL, D_MODEL)),
                  const((1, D_MODEL)), const((1, D_MODEL)), const((D_MODEL, N_EXPERTS))],
        out_specs=[row_d, row_d, pl.BlockSpec((tm, N_EXPERTS), lambda i: (i, 0))],
        out_shape=[jax.ShapeDtypeStruct((n, D_MODEL), F32), jax.ShapeDtypeStruct((n, D_MODEL), BF16),
                   jax.ShapeDtypeStruct((n, N_EXPERTS), F32)],
        scratch_shapes=[pltpu.VMEM((tm, HEAD_DIM), F32)] * 4,
        compiler_params=_cparams("parallel"),
        name="out_proj_ln_router",
    )(*o_groups, *lse_groups, mix_b, x2d, w_out_bf, ln_g.reshape(1, D_MODEL), ln_b.reshape(1, D_MODEL), w_router)


FFN_TM = 512
FFN_CHUNK = 256


def _ffn_kernel(x_ref, gate_ref, w1_ref, w3_ref, w2_ref, o_ref):
    x = x_ref[0]
    acc = jnp.zeros((x.shape[0], D_MODEL), F32)
    for c in range(D_FF // FFN_CHUNK):
        cols = slice(c * FFN_CHUNK, (c + 1) * FFN_CHUNK)
        g = jnp.dot(x, w1_ref[0, :, cols], preferred_element_type=F32)
        u = jnp.dot(x, w3_ref[0, :, cols], preferred_element_type=F32)
        h = (g * jax.nn.sigmoid(g) * u).astype(BF16)
        acc = acc + jnp.dot(h, w2_ref[0, cols, :], preferred_element_type=F32)
    o_ref[0] = acc * gate_ref[0]


def _expert_ffn(xe, gates, w1, w3, w2):
    e, rows, _ = xe.shape
    tm = FFN_TM
    return pl.pallas_call(
        _ffn_kernel,
        grid=(e, rows // tm),
        in_specs=[
            pl.BlockSpec((1, tm, D_MODEL), lambda ei, i: (ei, i, 0)),
            pl.BlockSpec((1, tm, 1), lambda ei, i: (ei, i, 0)),
            pl.BlockSpec((1, D_MODEL, D_FF), lambda ei, i: (ei, 0, 0)),
            pl.BlockSpec((1, D_MODEL, D_FF), lambda ei, i: (ei, 0, 0)),
            pl.BlockSpec((1, D_FF, D_MODEL), lambda ei, i: (ei, 0, 0)),
        ],
        out_specs=pl.BlockSpec((1, tm, D_MODEL), lambda ei, i: (ei, i, 0)),
        out_shape=jax.ShapeDtypeStruct((e, rows, D_MODEL), F32),
        compiler_params=_cparams("arbitrary", "arbitrary"),
        name="expert_ffn",
    )(xe, gates, w1, w3, w2)


LN_TM = 512


def _residual_ln_kernel(x_ref, f_ref, g_ref, b_ref, o_ref):
    o_ref[...] = _layer_norm(ALPHA * x_ref[...] + f_ref[...], g_ref[...], b_ref[...])


def _residual_ln(x2d, f2d, ln_g, ln_b):
    n = x2d.shape[0]
    row = pl.BlockSpec((LN_TM, D_MODEL), lambda i: (i, 0))
    const = pl.BlockSpec((1, D_MODEL), lambda i: (0, 0))
    return pl.pallas_call(
        _residual_ln_kernel,
        grid=(n // LN_TM,),
        in_specs=[row, row, const, const],
        out_specs=row,
        out_shape=jax.ShapeDtypeStruct((n, D_MODEL), F32),
        compiler_params=_cparams("parallel"),
        name="residual_ln",
    )(x2d, f2d, ln_g.reshape(1, D_MODEL), ln_b.reshape(1, D_MODEL))


def _trunk(x2d, group_rows, batch, seq, w_in, w_out, lambda_q1, lambda_k1, lambda_q2, lambda_k2, subln_g,
           ln1_g, ln1_b, w_router, w_gate, w_up, w_down, ln2_g, ln2_b):
    tables = _rope_tables(seq)
    depth = w_in.shape[0]
    for l in range(depth):
        lambda_init = 0.8 - 0.6 * math.exp(-0.3 * l)
        lam = (jnp.exp(jnp.sum(lambda_q1[l] * lambda_k1[l])) - jnp.exp(jnp.sum(lambda_q2[l] * lambda_k2[l]))
               + lambda_init)
        scalars = jnp.stack([lam, jnp.asarray(1.0 - lambda_init, F32)]).astype(F32)

        *qkv_groups, qkv_b = _in_proj(x2d, _prep_w_in(w_in[l]), tables, batch, seq)
        o_groups, lse_groups = [], []
        for g in range(len(DILATIONS)):
            o, lse = _dilated_group(qkv_groups[g], batch, seq, g)
            o_groups.append(o)
            lse_groups.append(lse)
        o_groups[0] = o_groups[0].reshape(batch * seq, WIDTH_A)
        lse_groups[0] = lse_groups[0].reshape(batch * seq, WIDTH_A)
        mix_b = _diff_attention(qkv_b, scalars, subln_g[l], batch, seq)
        x1, x1b, aff = _out_proj(o_groups, lse_groups, mix_b, x2d, w_out[l].astype(BF16), ln1_g[l], ln1_b[l],
                                 w_router[l], seq)

        xe_parts, gate_parts, idx_parts = [], [], []
        start = 0
        for rows in group_rows:
            cap = CAPACITY_FACTOR * rows // N_EXPERTS
            gates, idx = lax.top_k(aff[start:start + rows].T, cap)
            xe_parts.append(x1b[start:start + rows][idx])
            gate_parts.append(gates)
            idx_parts.append(idx + start)
            start += rows
        xe = jnp.concatenate(xe_parts, axis=1)
        gates = jnp.concatenate(gate_parts, axis=1)
        idx = jnp.concatenate(idx_parts, axis=1)
        ye = _expert_ffn(xe, gates[..., None], w_gate[l].astype(BF16), w_up[l].astype(BF16),
                         w_down[l].astype(BF16))
        f = jnp.zeros_like(x1).at[idx.reshape(-1)].add(ye.reshape(-1, D_MODEL))
        x2d = _residual_ln(x1, f, ln2_g[l], ln2_b[l])
    return x2d


def kernel(x_prompt, x_sample, w_in, w_out, lambda_q1, lambda_k1, lambda_q2, lambda_k2, subln_g, ln1_g, ln1_b,
           w_router, w_gate, w_up, w_down, ln2_g, ln2_b):
    bp, seq, d = x_prompt.shape
    bs = x_sample.shape[0]
    assert x_sample.shape[1] == seq
    x2d = jnp.concatenate([x_prompt.reshape(bp * seq, d), x_sample.reshape(bs * seq, d)], axis=0)
    y = _trunk(x2d, (bp * seq, bs * seq), bp + bs, seq, w_in, w_out, lambda_q1, lambda_k1, lambda_q2, lambda_k2,
               subln_g, ln1_g, ln1_b, w_router, w_gate, w_up, w_down, ln2_g, ln2_b)
    return y[:bp * seq].reshape(bp, seq, d), y[bp * seq:].reshape(bs, seq, d)
```

```python
import functools
import math

import jax
import jax.numpy as jnp
from jax import lax
from jax.experimental import pallas as pl
from jax.experimental.pallas import tpu as pltpu

D_MODEL = 1024
DEPTH = 4
DILATIONS = (1, 4, 16)
RADIUS = 64
HEADS_A = 4
HEAD_DIM = 128
WIDTH_A = HEADS_A * HEAD_DIM
QKV_A = len(DILATIONS) * WIDTH_A
HEADS_B = 4
SUB_DIM_B = 64
WIDTH_B = HEADS_B * HEAD_DIM
IN_COLS = 3 * QKV_A + 3 * WIDTH_B
N_EXPERTS = 16
D_FF = 2816
CAPACITY_FACTOR = 2
ROPE_THETA = 10000.0
LN_EPS = 1e-5
NEG = -1e30
ALPHA = (2 * DEPTH) ** 0.25
LOG2E = 1.4426950408889634
LN2 = 0.6931471805599453

QA_BLK, KA_BLK, VA_BLK = 0, QKV_A // HEAD_DIM, 2 * QKV_A // HEAD_DIM
QB_BLK = 3 * QKV_A // HEAD_DIM
KB_BLK = QB_BLK + WIDTH_B // HEAD_DIM
VB_BLK = KB_BLK + WIDTH_B // HEAD_DIM
N_BLKS = IN_COLS // HEAD_DIM

VMEM_LIMIT = 56 * 1024 * 1024
BF16 = jnp.bfloat16
F32 = jnp.float32


def _cparams(*sem):
    return pltpu.CompilerParams(dimension_semantics=sem, vmem_limit_bytes=VMEM_LIMIT)


IN_TM = 512
IN_CHUNK = 512


def _in_proj_kernel(x_ref, w_ref, tab_ref, a0_ref, a1_ref, a2_ref, b_ref, scr_ref):
    xb = x_ref[...].astype(BF16)
    tm = xb.shape[0]
    group_refs = (a0_ref, a1_ref, a2_ref)
    n_groups = len(DILATIONS)
    for ch in range(IN_COLS // IN_CHUNK):
        c0 = ch * IN_CHUNK
        acc = jnp.dot(xb, w_ref[:, c0:c0 + IN_CHUNK], preferred_element_type=F32)
        if ch < 3 * n_groups:
            sec, g = divmod(ch, n_groups)
            tab = (0, 2, None)[sec]
        else:
            sec, g = ch - 3 * n_groups, None
            tab = (4, 6, None)[sec]
        blocks = []
        for j in range(IN_CHUNK // HEAD_DIM):
            blk = acc[:, j * HEAD_DIM:(j + 1) * HEAD_DIM]
            if tab is not None:
                blk = blk * tab_ref[tab] + pltpu.roll(blk, HEAD_DIM // 2, 1) * tab_ref[tab + 1]
            blocks.append(blk)
        if g is None or g == 0:
            dst = b_ref if g is None else a0_ref
            for j, blk in enumerate(blocks):
                lo = sec * WIDTH_A + j * HEAD_DIM
                dst[:, lo:lo + HEAD_DIM] = blk.astype(BF16)
        else:
            r = DILATIONS[g]
            for j, blk in enumerate(blocks):
                scr_ref[j] = blk
            for c in range(r):
                for j in range(len(blocks)):
                    rows = scr_ref[j, pl.ds(c, tm // r, stride=r), :]
                    lo = c * 3 * WIDTH_A + sec * WIDTH_A + j * HEAD_DIM
                    group_refs[g][0, :, lo:lo + HEAD_DIM] = rows.astype(BF16)


def _in_proj(x2d, w_bf, tables, batch, seq):
    n = x2d.shape[0]
    tm = min(IN_TM, seq)
    tiles_per_seq = seq // tm
    r1, r2 = DILATIONS[1], DILATIONS[2]
    qkv = 3 * WIDTH_A
    seq_map = lambda i: (i // tiles_per_seq, i % tiles_per_seq, 0)
    return pl.pallas_call(
        _in_proj_kernel,
        grid=(n // tm,),
        in_specs=[
            pl.BlockSpec((tm, D_MODEL), lambda i: (i, 0)),
            pl.BlockSpec((D_MODEL, IN_COLS), lambda i: (0, 0), pipeline_mode=pl.Buffered(1)),
            pl.BlockSpec((8, tm, HEAD_DIM), lambda i: (0, i % tiles_per_seq, 0)),
        ],
        out_specs=[
            pl.BlockSpec((tm, qkv), lambda i: (i, 0)),
            pl.BlockSpec((1, tm // r1, r1 * qkv), seq_map),
            pl.BlockSpec((1, tm // r2, r2 * qkv), seq_map),
            pl.BlockSpec((tm, qkv), lambda i: (i, 0)),
        ],
        out_shape=[
            jax.ShapeDtypeStruct((n, qkv), BF16),
            jax.ShapeDtypeStruct((batch, seq // r1, r1 * qkv), BF16),
            jax.ShapeDtypeStruct((batch, seq // r2, r2 * qkv), BF16),
            jax.ShapeDtypeStruct((n, qkv), BF16),
        ],
        scratch_shapes=[pltpu.VMEM((IN_CHUNK // HEAD_DIM, tm, HEAD_DIM), F32)],
        compiler_params=_cparams("parallel"),
        name="in_proj_rope",
    )(x2d, w_bf, tables)


def _rope_tables(seq):
    pos = jnp.arange(seq, dtype=F32)[:, None]

    def cs(d):
        inv = ROPE_THETA ** (-jnp.arange(0, d, 2, dtype=F32) / d)
        ang = pos * inv[None, :]
        return jnp.cos(ang), jnp.sin(ang)

    ca, sa = cs(HEAD_DIM)
    cos_a = jnp.concatenate([ca, ca], axis=1)
    sin_a = jnp.concatenate([-sa, sa], axis=1)
    cb, sb = cs(SUB_DIM_B)
    cos_b = jnp.concatenate([cb, cb, cb, cb], axis=1)
    sin_b = jnp.concatenate([-sb, -sb, sb, sb], axis=1)
    qa = HEAD_DIM ** -0.5 * LOG2E
    qb = SUB_DIM_B ** -0.5 * LOG2E
    return jnp.stack([cos_a * qa, sin_a * qa, cos_a, sin_a, cos_b * qb, sin_b * qb, cos_b, sin_b], axis=0)


def _prep_w_in(w_in):
    half = SUB_DIM_B // 2
    perm = []
    for quarter in range(4):
        t, hi = quarter % 2, quarter // 2
        perm.extend(t * SUB_DIM_B + hi * half + i for i in range(half))
    perm = jnp.asarray(perm, dtype=jnp.int32)
    cols = jnp.arange(IN_COLS, dtype=jnp.int32)
    start, stop = QB_BLK * HEAD_DIM, VB_BLK * HEAD_DIM
    inside = (cols >= start) & (cols < stop)
    rel = cols - start
    permuted = start + (rel // HEAD_DIM) * HEAD_DIM + perm[rel % HEAD_DIM]
    src = jnp.where(inside, permuted, cols)
    return jnp.take(w_in, src, axis=1).astype(BF16)


A_QB = 128
A_WIN = A_QB + 2 * RADIUS
A_UNROLL = 8


def _dilated_kernel(q_ref, k_ref, v_ref, o_ref, lse_ref, *, sub_len, heads):
    nq = sub_len // A_QB
    row = lax.broadcasted_iota(jnp.int32, (A_QB, A_WIN), 0)
    col = lax.broadcasted_iota(jnp.int32, (A_QB, A_WIN), 1)
    rel0 = col - row

    for h in range(heads):
        lanes = slice(h * HEAD_DIM, (h + 1) * HEAD_DIM)

        def body(qi, carry, lanes=lanes):
            q0 = pl.multiple_of(qi * A_QB, A_QB)
            ws = pl.multiple_of(jnp.clip(q0 - RADIUS, 0, sub_len - A_WIN), RADIUS)
            qb = q_ref[0, pl.ds(q0, A_QB), lanes]
            kw = k_ref[0, pl.ds(ws, A_WIN), lanes]
            vw = v_ref[0, pl.ds(ws, A_WIN), lanes]
            s = lax.dot_general(qb, kw, (((1,), (1,)), ((), ())), preferred_element_type=F32)
            rel = rel0 - (q0 - ws)
            s = jnp.where(jnp.abs(rel) <= RADIUS, s, NEG)
            m = jnp.max(s, axis=1, keepdims=True)
            p = jnp.exp2(s - m)
            l = jnp.sum(p, axis=1, keepdims=True)
            o = jnp.dot(p.astype(BF16), vw, preferred_element_type=F32)
            o_ref[0, pl.ds(q0, A_QB), lanes] = o / l
            lse = (m + jnp.log2(l)) * LN2
            lse_ref[0, pl.ds(q0, A_QB), lanes] = jnp.broadcast_to(lse, (A_QB, HEAD_DIM))
            return carry

        lax.fori_loop(0, nq, body, 0, unroll=min(nq, A_UNROLL))


def _dilated_group(qkv, batch, seq, g):
    r = DILATIONS[g]
    sub_len = seq // r
    assert sub_len % A_QB == 0 and sub_len >= A_WIN
    heads = 1 if r == 1 else HEADS_A
    width = heads * HEAD_DIM
    steps = HEADS_A // heads
    per_sec = WIDTH_A // width
    view = qkv.reshape(batch, sub_len, r * 3 * WIDTH_A)

    def in_map(sec):
        return lambda b, c, hh: (b, 0, (c * 3 + sec) * per_sec + hh)

    out_map = lambda b, c, hh: (b, 0, c * steps + hh)
    blk = (1, sub_len, width)
    return pl.pallas_call(
        functools.partial(_dilated_kernel, sub_len=sub_len, heads=heads),
        grid=(batch, r, steps),
        in_specs=[pl.BlockSpec(blk, in_map(0)), pl.BlockSpec(blk, in_map(1)), pl.BlockSpec(blk, in_map(2))],
        out_specs=[pl.BlockSpec(blk, out_map), pl.BlockSpec(blk, out_map)],
        out_shape=[jax.ShapeDtypeStruct((batch, sub_len, r * WIDTH_A), F32)] * 2,
        compiler_params=_cparams("parallel", "parallel", "parallel"),
        name=f"dilated_attn_r{r}",
    )(view, view, view)


B_TQ = 512
B_TK = 512
B_UNROLL = 4


def _diff_kernel(scal_ref, q_ref, k_ref, vt_ref, g_ref, o_ref, *s_scrs, seq):
    lam = scal_ref[0]
    out_scale = scal_ref[1]
    q = q_ref[0]
    tq = q.shape[0]
    lane = lax.broadcasted_iota(jnp.int32, q.shape, 1)
    first = ((lane // (SUB_DIM_B // 2)) % 2) == 0
    zero = jnp.zeros_like(q)
    nk = seq // B_TK
    qs = (jnp.where(first, q, zero), jnp.where(first, zero, q))

    def scores(j, t, m):
        k0 = pl.multiple_of(j * B_TK, B_TK)
        s = lax.dot_general(k_ref[0, pl.ds(k0, B_TK), :], qs[t], (((1,), (1,)), ((), ())),
                            preferred_element_type=F32)
        s_scrs[t][pl.ds(k0, B_TK), :] = s
        return jnp.maximum(m, jnp.max(s, axis=0, keepdims=True))

    def weigh(j, t, m, l, acc):
        k0 = pl.multiple_of(j * B_TK, B_TK)
        p = jnp.exp2(s_scrs[t][pl.ds(k0, B_TK), :] - m)
        l = l + jnp.sum(p, axis=0, keepdims=True)
        acc = acc + jnp.dot(vt_ref[0, 0, :, pl.ds(k0, B_TK)], p.astype(BF16), preferred_element_type=F32)
        return l, acc

    m_init = jnp.full((1, tq), -jnp.inf, F32)
    la_init = (jnp.zeros((1, tq), F32), jnp.zeros((HEAD_DIM, tq), F32))
    m0 = lax.fori_loop(0, nk, lambda j, m: scores(j, 0, m), m_init, unroll=B_UNROLL)

    def middle(j, carry):
        m1, l0, acc0 = carry
        l0, acc0 = weigh(j, 0, m0, l0, acc0)
        return scores(j, 1, m1), l0, acc0

    m1, l0, acc0 = lax.fori_loop(0, nk, middle, (m_init,) + la_init, unroll=B_UNROLL)
    l1, acc1 = lax.fori_loop(0, nk, lambda j, c: weigh(j, 1, m1, *c), la_init, unroll=B_UNROLL)
    o = acc0 / l0 - lam * (acc1 / l1)
    o = o * lax.rsqrt(jnp.mean(o * o, axis=0, keepdims=True) + LN_EPS) * g_ref[...]
    o_ref[0] = (o * out_scale).T.astype(BF16)


def _diff_attention(qkv_b, scalars, subln_g, batch, seq):
    view = qkv_b.reshape(batch, seq, 3 * WIDTH_B)
    v_t = view[:, :, 2 * WIDTH_B:].reshape(batch, seq, HEADS_B, HEAD_DIM).transpose(0, 2, 3, 1)
    tq = min(B_TQ, seq)
    out = pl.pallas_call(
        functools.partial(_diff_kernel, seq=seq),
        grid=(batch, HEADS_B, seq // tq),
        in_specs=[
            pl.BlockSpec(memory_space=pltpu.SMEM),
            pl.BlockSpec((1, tq, HEAD_DIM), lambda b, h, i: (b, i, h)),
            pl.BlockSpec((1, seq, HEAD_DIM), lambda b, h, i: (b, 0, HEADS_B + h)),
            pl.BlockSpec((1, 1, HEAD_DIM, seq), lambda b, h, i: (b, h, 0, 0)),
            pl.BlockSpec((HEAD_DIM, 1), lambda b, h, i: (0, 0)),
        ],
        out_specs=pl.BlockSpec((1, tq, HEAD_DIM), lambda b, h, i: (b, i, h)),
        out_shape=jax.ShapeDtypeStruct((batch, seq, WIDTH_B), BF16),
        scratch_shapes=[pltpu.VMEM((seq, tq), F32)] * 2,
        compiler_params=_cparams("parallel", "parallel", "arbitrary"),
        name="diff_attn",
    )(scalars, view, view, v_t, subln_g.reshape(HEAD_DIM, 1).astype(F32))
    return out.reshape(batch * seq, WIDTH_B)


OUT_TM = 256


def _layer_norm(y, g, b):
    mu = jnp.mean(y, axis=1, keepdims=True)
    d = y - mu
    var = jnp.mean(d * d, axis=1, keepdims=True)
    return d * lax.rsqrt(var + LN_EPS) * g + b


def _token_order(ref, r, head, scr):
    rows = scr.shape[0]
    for c in range(r):
        lo = c * WIDTH_A + head * HEAD_DIM
        scr[pl.ds(c, rows // r, stride=r), :] = ref[0, :, lo:lo + HEAD_DIM]
    return scr[...]


def _out_proj_kernel(o0, o1, o2, l0, l1, l2, mb_ref, x_ref, w_ref, g_ref, b_ref, wr_ref,
                     x1_ref, x1b_ref, aff_ref, so1, so2, sl1, sl2):
    r1, r2 = DILATIONS[1], DILATIONS[2]
    heads = []
    for h in range(HEADS_A):
        lanes = slice(h * HEAD_DIM, (h + 1) * HEAD_DIM)
        la, lb, lc = l0[:, lanes], _token_order(l1, r1, h, sl1), _token_order(l2, r2, h, sl2)
        mx = jnp.maximum(jnp.maximum(la, lb), lc)
        ea, eb, ec = jnp.exp(la - mx), jnp.exp(lb - mx), jnp.exp(lc - mx)
        num = ea * o0[:, lanes] + eb * _token_order(o1, r1, h, so1) + ec * _token_order(o2, r2, h, so2)
        heads.append((num / (ea + eb + ec)).astype(BF16))
    mix_a = jnp.concatenate(heads, axis=1)
    m = jnp.dot(mix_a, w_ref[:WIDTH_A, :], preferred_element_type=F32)
    m = m + jnp.dot(mb_ref[...], w_ref[WIDTH_A:, :], preferred_element_type=F32)
    x1 = _layer_norm(ALPHA * x_ref[...] + m, g_ref[...], b_ref[...])
    x1_ref[...] = x1
    x1b_ref[...] = x1.astype(BF16)
    logits = jnp.dot(x1, wr_ref[...], preferred_element_type=F32, precision=lax.Precision.HIGHEST)
    e = jnp.exp(logits - jnp.max(logits, axis=1, keepdims=True))
    aff_ref[...] = e / jnp.sum(e, axis=1, keepdims=True)


def _out_proj(o_groups, lse_groups, mix_b, x2d, w_out_bf, ln_g, ln_b, w_router, seq):
    n = x2d.shape[0]
    tm = OUT_TM
    tiles_per_seq = seq // tm
    r1, r2 = DILATIONS[1], DILATIONS[2]
    row_a = pl.BlockSpec((tm, WIDTH_A), lambda i: (i, 0))
    row_d = pl.BlockSpec((tm, D_MODEL), lambda i: (i, 0))
    seq_map = lambda i: (i // tiles_per_seq, i % tiles_per_seq, 0)
    res1 = pl.BlockSpec((1, tm // r1, r1 * WIDTH_A), seq_map)
    res2 = pl.BlockSpec((1, tm // r2, r2 * WIDTH_A), seq_map)
    const = lambda shape: pl.BlockSpec(shape, lambda i: (0, 0))
    return pl.pallas_call(
        _out_proj_kernel,
        grid=(n // tm,),
        in_specs=[row_a, res1, res2, row_a, res1, res2, row_a, row_d, const((D_MODEL, D_MODEL)),
                  const((1, D_MODEL)), const((1, D_MODEL)), const((D_MODEL, N_EXPERTS))],
        out_specs=[row_d, row_d, pl.BlockSpec((tm, N_EXPERTS), lambda i: (i, 0))],
        out_shape=[jax.ShapeDtypeStruct((n, D_MODEL), F32), jax.ShapeDtypeStruct((n, D_MODEL), BF16),
                   jax.ShapeDtypeStruct((n, N_EXPERTS), F32)],
        scratch_shapes=[pltpu.VMEM((tm, HEAD_DIM), F32)] * 4,
        compiler_params=_cparams("parallel"),
        name="out_proj_ln_router",
    )(*o_groups, *lse_groups, mix_b, x2d, w_out_bf, ln_g.reshape(1, D_MODEL), ln_b.reshape(1, D_MODEL), w_router)


ONE_BITS = 0x3F800000


def _threshold_kernel(aff_ref, thr_ref, *, cap):
    bits = pltpu.bitcast(aff_ref[...], jnp.int32)
    e = bits.shape[0]

    def body(_, carry):
        lo, hi = carry
        mid = lo + (hi - lo) // 2
        cnt = jnp.sum((bits >= mid).astype(jnp.int32), axis=1, keepdims=True)
        take = cnt >= cap
        return jnp.where(take, mid, lo), jnp.where(take, hi, mid)

    lo, _ = lax.fori_loop(0, 31, body, (jnp.zeros((e, 1), jnp.int32), jnp.full((e, 1), ONE_BITS + 1, jnp.int32)))
    thr_ref[...] = jnp.broadcast_to(lo, thr_ref.shape)


def _expert_choice(aff_t, cap):
    e, n = aff_t.shape
    thr = pl.pallas_call(
        functools.partial(_threshold_kernel, cap=cap),
        out_shape=jax.ShapeDtypeStruct((e, HEAD_DIM), jnp.int32),
        compiler_params=pltpu.CompilerParams(vmem_limit_bytes=VMEM_LIMIT),
        name="expert_threshold",
    )(aff_t)[:, :1]
    bits = lax.bitcast_convert_type(aff_t, jnp.int32)
    above = bits > thr
    at = bits == thr
    need = cap - jnp.sum(above, axis=1, keepdims=True)
    picked = above | (at & (jnp.cumsum(at, axis=1) <= need))
    rank = jnp.cumsum(picked.astype(jnp.int32), axis=1)
    slots = jnp.arange(1, cap + 1, dtype=jnp.int32)
    idx = jax.vmap(lambda row: jnp.searchsorted(row, slots, side="left"))(rank).astype(jnp.int32)
    return jnp.take_along_axis(aff_t, idx, axis=1), idx


FFN_TM = 512
FFN_CHUNK = 256


def _ffn_kernel(x_ref, gate_ref, w1_ref, w3_ref, w2_ref, o_ref):
    x = x_ref[0]
    acc = jnp.zeros((x.shape[0], D_MODEL), F32)
    for c in range(D_FF // FFN_CHUNK):
        cols = slice(c * FFN_CHUNK, (c + 1) * FFN_CHUNK)
        g = jnp.dot(x, w1_ref[0, :, cols], preferred_element_type=F32)
        u = jnp.dot(x, w3_ref[0, :, cols], preferred_element_type=F32)
        h = (g * jax.nn.sigmoid(g) * u).astype(BF16)
        acc = acc + jnp.dot(h, w2_ref[0, cols, :], preferred_element_type=F32)
    o_ref[0] = acc * gate_ref[0]


def _expert_ffn(xe, gates, w1, w3, w2):
    e, rows, _ = xe.shape
    tm = FFN_TM
    return pl.pallas_call(
        _ffn_kernel,
        grid=(e, rows // tm),
        in_specs=[
            pl.BlockSpec((1, tm, D_MODEL), lambda ei, i: (ei, i, 0)),
            pl.BlockSpec((1, tm, 1), lambda ei, i: (ei, i, 0)),
            pl.BlockSpec((1, D_MODEL, D_FF), lambda ei, i: (ei, 0, 0)),
            pl.BlockSpec((1, D_MODEL, D_FF), lambda ei, i: (ei, 0, 0)),
            pl.BlockSpec((1, D_FF, D_MODEL), lambda ei, i: (ei, 0, 0)),
        ],
        out_specs=pl.BlockSpec((1, tm, D_MODEL), lambda ei, i: (ei, i, 0)),
        out_shape=jax.ShapeDtypeStruct((e, rows, D_MODEL), F32),
        compiler_params=_cparams("arbitrary", "arbitrary"),
        name="expert_ffn",
    )(xe, gates, w1, w3, w2)


LN_TM = 512


def _residual_ln_kernel(x_ref, f_ref, g_ref, b_ref, o_ref):
    o_ref[...] = _layer_norm(ALPHA * x_ref[...] + f_ref[...], g_ref[...], b_ref[...])


def _residual_ln(x2d, f2d, ln_g, ln_b):
    n = x2d.shape[0]
    row = pl.BlockSpec((LN_TM, D_MODEL), lambda i: (i, 0))
    const = pl.BlockSpec((1, D_MODEL), lambda i: (0, 0))
    return pl.pallas_call(
        _residual_ln_kernel,
        grid=(n // LN_TM,),
        in_specs=[row, row, const, const],
        out_specs=row,
        out_shape=jax.ShapeDtypeStruct((n, D_MODEL), F32),
        compiler_params=_cparams("parallel"),
        name="residual_ln",
    )(x2d, f2d, ln_g.reshape(1, D_MODEL), ln_b.reshape(1, D_MODEL))


def _trunk(x2d, group_rows, batch, seq, w_in, w_out, lambda_q1, lambda_k1, lambda_q2, lambda_k2, subln_g,
           ln1_g, ln1_b, w_router, w_gate, w_up, w_down, ln2_g, ln2_b):
    tables = _rope_tables(seq)
    depth = w_in.shape[0]
    for l in range(depth):
        lambda_init = 0.8 - 0.6 * math.exp(-0.3 * l)
        lam = (jnp.exp(jnp.sum(lambda_q1[l] * lambda_k1[l])) - jnp.exp(jnp.sum(lambda_q2[l] * lambda_k2[l]))
               + lambda_init)
        scalars = jnp.stack([lam, jnp.asarray(1.0 - lambda_init, F32)]).astype(F32)

        *qkv_groups, qkv_b = _in_proj(x2d, _prep_w_in(w_in[l]), tables, batch, seq)
        o_groups, lse_groups = [], []
        for g in range(len(DILATIONS)):
            o, lse = _dilated_group(qkv_groups[g], batch, seq, g)
            o_groups.append(o)
            lse_groups.append(lse)
        o_groups[0] = o_groups[0].reshape(batch * seq, WIDTH_A)
        lse_groups[0] = lse_groups[0].reshape(batch * seq, WIDTH_A)
        mix_b = _diff_attention(qkv_b, scalars, subln_g[l], batch, seq)
        x1, x1b, aff = _out_proj(o_groups, lse_groups, mix_b, x2d, w_out[l].astype(BF16), ln1_g[l], ln1_b[l],
                                 w_router[l], seq)

        xe_parts, gate_parts, idx_parts = [], [], []
        start = 0
        for rows in group_rows:
            cap = CAPACITY_FACTOR * rows // N_EXPERTS
            gates, idx = _expert_choice(aff[start:start + rows].T, cap)
            xe_parts.append(x1b[start:start + rows][idx])
            gate_parts.append(gates)
            idx_parts.append(idx + start)
            start += rows
        xe = jnp.concatenate(xe_parts, axis=1)
        gates = jnp.concatenate(gate_parts, axis=1)
        idx = jnp.concatenate(idx_parts, axis=1)
        ye = _expert_ffn(xe, gates[..., None], w_gate[l].astype(BF16), w_up[l].astype(BF16),
                         w_down[l].astype(BF16))
        f = jnp.zeros_like(x1).at[idx.reshape(-1)].add(ye.reshape(-1, D_MODEL))
        x2d = _residual_ln(x1, f, ln2_g[l], ln2_b[l])
    return x2d


def kernel(x_prompt, x_sample, w_in, w_out, lambda_q1, lambda_k1, lambda_q2, lambda_k2, subln_g, ln1_g, ln1_b,
           w_router, w_gate, w_up, w_down, ln2_g, ln2_b):
    bp, seq, d = x_prompt.shape
    bs = x_sample.shape[0]
    assert x_sample.shape[1] == seq
    x2d = jnp.concatenate([x_prompt.reshape(bp * seq, d), x_sample.reshape(bs * seq, d)], axis=0)
    y = _trunk(x2d, (bp * seq, bs * seq), bp + bs, seq, w_in, w_out, lambda_q1, lambda_k1, lambda_q2, lambda_k2,
               subln_g, ln1_g, ln1_b, w_router, w_gate, w_up, w_down, ln2_g, ln2_b)
    return y[:bp * seq].reshape(bp, seq, d), y[bp * seq:].reshape(bs, seq, d)
```

```python
import functools
import math

import jax
import jax.numpy as jnp
from jax import lax
from jax.experimental import pallas as pl
from jax.experimental.pallas import tpu as pltpu

D_MODEL = 1024
DEPTH = 4
DILATIONS = (1, 4, 16)
RADIUS = 64
HEADS_A = 4
HEAD_DIM = 128
WIDTH_A = HEADS_A * HEAD_DIM
QKV_A = len(DILATIONS) * WIDTH_A
HEADS_B = 4
SUB_DIM_B = 64
WIDTH_B = HEADS_B * HEAD_DIM
IN_COLS = 3 * QKV_A + 3 * WIDTH_B
N_EXPERTS = 16
D_FF = 2816
CAPACITY_FACTOR = 2
ROPE_THETA = 10000.0
LN_EPS = 1e-5
NEG = -1e30
ALPHA = (2 * DEPTH) ** 0.25
LOG2E = 1.4426950408889634
LN2 = 0.6931471805599453

QA_BLK, KA_BLK, VA_BLK = 0, QKV_A // HEAD_DIM, 2 * QKV_A // HEAD_DIM
QB_BLK = 3 * QKV_A // HEAD_DIM
KB_BLK = QB_BLK + WIDTH_B // HEAD_DIM
VB_BLK = KB_BLK + WIDTH_B // HEAD_DIM
N_BLKS = IN_COLS // HEAD_DIM

VMEM_LIMIT = 56 * 1024 * 1024
BF16 = jnp.bfloat16
F32 = jnp.float32


def _cparams(*sem):
    return pltpu.CompilerParams(dimension_semantics=sem, vmem_limit_bytes=VMEM_LIMIT)


IN_TM = 512
IN_CHUNK = 512


def _in_proj_kernel(x_ref, w_ref, tab_ref, a0_ref, a1_ref, a2_ref, b_ref, scr_ref):
    xb = x_ref[...].astype(BF16)
    tm = xb.shape[0]
    group_refs = (a0_ref, a1_ref, a2_ref)
    n_groups = len(DILATIONS)
    for ch in range(IN_COLS // IN_CHUNK):
        c0 = ch * IN_CHUNK
        acc = jnp.dot(xb, w_ref[:, c0:c0 + IN_CHUNK], preferred_element_type=F32)
        if ch < 3 * n_groups:
            sec, g = divmod(ch, n_groups)
            tab = (0, 2, None)[sec]
        else:
            sec, g = ch - 3 * n_groups, None
            tab = (4, 6, None)[sec]
        blocks = []
        for j in range(IN_CHUNK // HEAD_DIM):
            blk = acc[:, j * HEAD_DIM:(j + 1) * HEAD_DIM]
            if tab is not None:
                blk = blk * tab_ref[tab] + pltpu.roll(blk, HEAD_DIM // 2, 1) * tab_ref[tab + 1]
            blocks.append(blk)
        if g is None or g == 0:
            dst = b_ref if g is None else a0_ref
            for j, blk in enumerate(blocks):
                lo = sec * WIDTH_A + j * HEAD_DIM
                dst[:, lo:lo + HEAD_DIM] = blk.astype(BF16)
        else:
            r = DILATIONS[g]
            for j, blk in enumerate(blocks):
                scr_ref[j] = blk
            for c in range(r):
                for j in range(len(blocks)):
                    rows = scr_ref[j, pl.ds(c, tm // r, stride=r), :]
                    lo = c * 3 * WIDTH_A + sec * WIDTH_A + j * HEAD_DIM
                    group_refs[g][0, :, lo:lo + HEAD_DIM] = rows.astype(BF16)


def _in_proj(x2d, w_bf, tables, batch, seq):
    n = x2d.shape[0]
    tm = min(IN_TM, seq)
    tiles_per_seq = seq // tm
    r1, r2 = DILATIONS[1], DILATIONS[2]
    qkv = 3 * WIDTH_A
    seq_map = lambda i: (i // tiles_per_seq, i % tiles_per_seq, 0)
    return pl.pallas_call(
        _in_proj_kernel,
        grid=(n // tm,),
        in_specs=[
            pl.BlockSpec((tm, D_MODEL), lambda i: (i, 0)),
            pl.BlockSpec((D_MODEL, IN_COLS), lambda i: (0, 0), pipeline_mode=pl.Buffered(1)),
            pl.BlockSpec((8, tm, HEAD_DIM), lambda i: (0, i % tiles_per_seq, 0)),
        ],
        out_specs=[
            pl.BlockSpec((tm, qkv), lambda i: (i, 0)),
            pl.BlockSpec((1, tm // r1, r1 * qkv), seq_map),
            pl.BlockSpec((1, tm // r2, r2 * qkv), seq_map),
            pl.BlockSpec((tm, qkv), lambda i: (i, 0)),
        ],
        out_shape=[
            jax.ShapeDtypeStruct((n, qkv), BF16),
            jax.ShapeDtypeStruct((batch, seq // r1, r1 * qkv), BF16),
            jax.ShapeDtypeStruct((batch, seq // r2, r2 * qkv), BF16),
            jax.ShapeDtypeStruct((n, qkv), BF16),
        ],
        scratch_shapes=[pltpu.VMEM((IN_CHUNK // HEAD_DIM, tm, HEAD_DIM), F32)],
        compiler_params=_cparams("parallel"),
        name="in_proj_rope",
    )(x2d, w_bf, tables)


def _rope_tables(seq):
    pos = jnp.arange(seq, dtype=F32)[:, None]

    def cs(d):
        inv = ROPE_THETA ** (-jnp.arange(0, d, 2, dtype=F32) / d)
        ang = pos * inv[None, :]
        return jnp.cos(ang), jnp.sin(ang)

    ca, sa = cs(HEAD_DIM)
    cos_a = jnp.concatenate([ca, ca], axis=1)
    sin_a = jnp.concatenate([-sa, sa], axis=1)
    cb, sb = cs(SUB_DIM_B)
    cos_b = jnp.concatenate([cb, cb, cb, cb], axis=1)
    sin_b = jnp.concatenate([-sb, -sb, sb, sb], axis=1)
    qa = HEAD_DIM ** -0.5 * LOG2E
    qb = SUB_DIM_B ** -0.5 * LOG2E
    return jnp.stack([cos_a * qa, sin_a * qa, cos_a, sin_a, cos_b * qb, sin_b * qb, cos_b, sin_b], axis=0)


def _prep_w_in(w_in):
    half = SUB_DIM_B // 2
    perm = []
    for quarter in range(4):
        t, hi = quarter % 2, quarter // 2
        perm.extend(t * SUB_DIM_B + hi * half + i for i in range(half))
    perm = jnp.asarray(perm, dtype=jnp.int32)
    cols = jnp.arange(IN_COLS, dtype=jnp.int32)
    start, stop = QB_BLK * HEAD_DIM, VB_BLK * HEAD_DIM
    inside = (cols >= start) & (cols < stop)
    rel = cols - start
    permuted = start + (rel // HEAD_DIM) * HEAD_DIM + perm[rel % HEAD_DIM]
    src = jnp.where(inside, permuted, cols)
    return jnp.take(w_in, src, axis=1).astype(BF16)


A_QB = 128
A_WIN = A_QB + 2 * RADIUS
A_UNROLL = 8


def _dilated_kernel(q_ref, k_ref, v_ref, o_ref, lse_ref, *, sub_len, heads):
    nq = sub_len // A_QB
    row = lax.broadcasted_iota(jnp.int32, (A_QB, A_WIN), 0)
    col = lax.broadcasted_iota(jnp.int32, (A_QB, A_WIN), 1)
    rel0 = col - row

    for h in range(heads):
        lanes = slice(h * HEAD_DIM, (h + 1) * HEAD_DIM)

        def body(qi, carry, lanes=lanes):
            q0 = pl.multiple_of(qi * A_QB, A_QB)
            ws = pl.multiple_of(jnp.clip(q0 - RADIUS, 0, sub_len - A_WIN), RADIUS)
            qb = q_ref[0, pl.ds(q0, A_QB), lanes]
            kw = k_ref[0, pl.ds(ws, A_WIN), lanes]
            vw = v_ref[0, pl.ds(ws, A_WIN), lanes]
            s = lax.dot_general(qb, kw, (((1,), (1,)), ((), ())), preferred_element_type=F32)
            rel = rel0 - (q0 - ws)
            s = jnp.where(jnp.abs(rel) <= RADIUS, s, NEG)
            m = jnp.max(s, axis=1, keepdims=True)
            p = jnp.exp2(s - m)
            l = jnp.sum(p, axis=1, keepdims=True)
            o = jnp.dot(p.astype(BF16), vw, preferred_element_type=F32)
            o_ref[0, pl.ds(q0, A_QB), lanes] = o / l
            lse = (m + jnp.log2(l)) * LN2
            lse_ref[0, pl.ds(q0, A_QB), lanes] = jnp.broadcast_to(lse, (A_QB, HEAD_DIM))
            return carry

        lax.fori_loop(0, nq, body, 0, unroll=min(nq, A_UNROLL))


def _dilated_group(qkv, batch, seq, g):
    r = DILATIONS[g]
    sub_len = seq // r
    assert sub_len % A_QB == 0 and sub_len >= A_WIN
    heads = 1 if r == 1 else HEADS_A
    width = heads * HEAD_DIM
    steps = HEADS_A // heads
    per_sec = WIDTH_A // width
    view = qkv.reshape(batch, sub_len, r * 3 * WIDTH_A)

    def in_map(sec):
        return lambda b, c, hh: (b, 0, (c * 3 + sec) * per_sec + hh)

    out_map = lambda b, c, hh: (b, 0, c * steps + hh)
    blk = (1, sub_len, width)
    return pl.pallas_call(
        functools.partial(_dilated_kernel, sub_len=sub_len, heads=heads),
        grid=(batch, r, steps),
        in_specs=[pl.BlockSpec(blk, in_map(0)), pl.BlockSpec(blk, in_map(1)), pl.BlockSpec(blk, in_map(2))],
        out_specs=[pl.BlockSpec(blk, out_map), pl.BlockSpec(blk, out_map)],
        out_shape=[jax.ShapeDtypeStruct((batch, sub_len, r * WIDTH_A), F32)] * 2,
        compiler_params=_cparams("parallel", "parallel", "parallel"),
        name=f"dilated_attn_r{r}",
    )(view, view, view)


B_TQ = 512
B_TK = 512
B_UNROLL = 4


def _diff_kernel(scal_ref, q_ref, k_ref, vt_ref, g_ref, o_ref, *s_scrs, seq):
    lam = scal_ref[0]
    out_scale = scal_ref[1]
    q = q_ref[0]
    tq = q.shape[0]
    lane = lax.broadcasted_iota(jnp.int32, q.shape, 1)
    first = ((lane // (SUB_DIM_B // 2)) % 2) == 0
    zero = jnp.zeros_like(q)
    nk = seq // B_TK
    qs = (jnp.where(first, q, zero), jnp.where(first, zero, q))

    def scores(j, t, m):
        k0 = pl.multiple_of(j * B_TK, B_TK)
        s = lax.dot_general(k_ref[0, pl.ds(k0, B_TK), :], qs[t], (((1,), (1,)), ((), ())),
                            preferred_element_type=F32)
        s_scrs[t][pl.ds(k0, B_TK), :] = s
        return jnp.maximum(m, jnp.max(s, axis=0, keepdims=True))

    def weigh(j, t, m, l, acc):
        k0 = pl.multiple_of(j * B_TK, B_TK)
        p = jnp.exp2(s_scrs[t][pl.ds(k0, B_TK), :] - m)
        l = l + jnp.sum(p, axis=0, keepdims=True)
        acc = acc + jnp.dot(vt_ref[0, 0, :, pl.ds(k0, B_TK)], p.astype(BF16), preferred_element_type=F32)
        return l, acc

    m_init = jnp.full((1, tq), -jnp.inf, F32)
    la_init = (jnp.zeros((1, tq), F32), jnp.zeros((HEAD_DIM, tq), F32))
    m0 = lax.fori_loop(0, nk, lambda j, m: scores(j, 0, m), m_init, unroll=B_UNROLL)

    def middle(j, carry):
        m1, l0, acc0 = carry
        l0, acc0 = weigh(j, 0, m0, l0, acc0)
        return scores(j, 1, m1), l0, acc0

    m1, l0, acc0 = lax.fori_loop(0, nk, middle, (m_init,) + la_init, unroll=B_UNROLL)
    l1, acc1 = lax.fori_loop(0, nk, lambda j, c: weigh(j, 1, m1, *c), la_init, unroll=B_UNROLL)
    o = acc0 / l0 - lam * (acc1 / l1)
    o = o * lax.rsqrt(jnp.mean(o * o, axis=0, keepdims=True) + LN_EPS) * g_ref[...]
    o_ref[0] = (o * out_scale).T.astype(BF16)


def _diff_attention(qkv_b, scalars, subln_g, batch, seq):
    view = qkv_b.reshape(batch, seq, 3 * WIDTH_B)
    v_t = view[:, :, 2 * WIDTH_B:].reshape(batch, seq, HEADS_B, HEAD_DIM).transpose(0, 2, 3, 1)
    tq = min(B_TQ, seq)
    out = pl.pallas_call(
        functools.partial(_diff_kernel, seq=seq),
        grid=(batch, HEADS_B, seq // tq),
        in_specs=[
            pl.BlockSpec(memory_space=pltpu.SMEM),
            pl.BlockSpec((1, tq, HEAD_DIM), lambda b, h, i: (b, i, h)),
            pl.BlockSpec((1, seq, HEAD_DIM), lambda b, h, i: (b, 0, HEADS_B + h)),
            pl.BlockSpec((1, 1, HEAD_DIM, seq), lambda b, h, i: (b, h, 0, 0)),
            pl.BlockSpec((HEAD_DIM, 1), lambda b, h, i: (0, 0)),
        ],
        out_specs=pl.BlockSpec((1, tq, HEAD_DIM), lambda b, h, i: (b, i, h)),
        out_shape=jax.ShapeDtypeStruct((batch, seq, WIDTH_B), BF16),
        scratch_shapes=[pltpu.VMEM((seq, tq), F32)] * 2,
        compiler_params=_cparams("parallel", "parallel", "arbitrary"),
        name="diff_attn",
    )(scalars, view, view, v_t, subln_g.reshape(HEAD_DIM, 1).astype(F32))
    return out.reshape(batch * seq, WIDTH_B)


OUT_TM = 256


def _layer_norm(y, g, b):
    mu = jnp.mean(y, axis=1, keepdims=True)
    d = y - mu
    var = jnp.mean(d * d, axis=1, keepdims=True)
    return d * lax.rsqrt(var + LN_EPS) * g + b


def _token_order(ref, r, head, scr):
    rows = scr.shape[0]
    for c in range(r):
        lo = c * WIDTH_A + head * HEAD_DIM
        scr[pl.ds(c, rows // r, stride=r), :] = ref[0, :, lo:lo + HEAD_DIM]
    return scr[...]


def _out_proj_kernel(o0, o1, o2, l0, l1, l2, mb_ref, x_ref, w_ref, g_ref, b_ref, wr_ref,
                     x1_ref, x1b_ref, aff_ref, so1, so2, sl1, sl2):
    r1, r2 = DILATIONS[1], DILATIONS[2]
    heads = []
    for h in range(HEADS_A):
        lanes = slice(h * HEAD_DIM, (h + 1) * HEAD_DIM)
        la, lb, lc = l0[:, lanes], _token_order(l1, r1, h, sl1), _token_order(l2, r2, h, sl2)
        mx = jnp.maximum(jnp.maximum(la, lb), lc)
        ea, eb, ec = jnp.exp(la - mx), jnp.exp(lb - mx), jnp.exp(lc - mx)
        num = ea * o0[:, lanes] + eb * _token_order(o1, r1, h, so1) + ec * _token_order(o2, r2, h, so2)
        heads.append((num / (ea + eb + ec)).astype(BF16))
    mix_a = jnp.concatenate(heads, axis=1)
    m = jnp.dot(mix_a, w_ref[:WIDTH_A, :], preferred_element_type=F32)
    m = m + jnp.dot(mb_ref[...], w_ref[WIDTH_A:, :], preferred_element_type=F32)
    x1 = _layer_norm(ALPHA * x_ref[...] + m, g_ref[...], b_ref[...])
    x1_ref[...] = x1
    x1b_ref[...] = x1.astype(BF16)
    logits = jnp.dot(x1, wr_ref[...], preferred_element_type=F32, precision=lax.Precision.HIGHEST)
    e = jnp.exp(logits - jnp.max(logits, axis=1, keepdims=True))
    aff_ref[...] = e / jnp.sum(e, axis=1, keepdims=True)


def _out_proj(o_groups, lse_groups, mix_b, x2d, w_out_bf, ln_g, ln_b, w_router, seq):
    n = x2d.shape[0]
    tm = OUT_TM
    tiles_per_seq = seq // tm
    r1, r2 = DILATIONS[1], DILATIONS[2]
    row_a = pl.BlockSpec((tm, WIDTH_A), lambda i: (i, 0))
    row_d = pl.BlockSpec((tm, D_MODEL), lambda i: (i, 0))
    seq_map = lambda i: (i // tiles_per_seq, i % tiles_per_seq, 0)
    res1 = pl.BlockSpec((1, tm // r1, r1 * WIDTH_A), seq_map)
    res2 = pl.BlockSpec((1, tm // r2, r2 * WIDTH_A), seq_map)
    const = lambda shape: pl.BlockSpec(shape, lambda i: (0, 0))
    return pl.pallas_call(
        _out_proj_kernel,
        grid=(n // tm,),
        in_specs=[row_a, res1, res2, row_a, res1, res2, row_a, row_d, const((D_MODEL, D_MODEL)),
                  const((1, D_MODEL)), const((1, D_MODEL)), const((D_MODEL, N_EXPERTS))],
        out_specs=[row_d, row_d, pl.BlockSpec((tm, N_EXPERTS), lambda i: (i, 0))],
        out_shape=[jax.ShapeDtypeStruct((n, D_MODEL), F32), jax.ShapeDtypeStruct((n, D_MODEL), BF16),
                   jax.ShapeDtypeStruct((n, N_EXPERTS), F32)],
        scratch_shapes=[pltpu.VMEM((tm, HEAD_DIM), F32)] * 4,
        compiler_params=_cparams("parallel"),
        name="out_proj_ln_router",
    )(*o_groups, *lse_groups, mix_b, x2d, w_out_bf, ln_g.reshape(1, D_MODEL), ln_b.reshape(1, D_MODEL), w_router)


ONE_BITS = 0x3F800000


def _threshold_kernel(aff_ref, thr_ref, *, cap):
    bits = pltpu.bitcast(aff_ref[...], jnp.int32)
    e = bits.shape[0]

    def body(_, carry):
        lo, hi = carry
        mid = lo + (hi - lo) // 2
        cnt = jnp.sum((bits >= mid).astype(jnp.int32), axis=1, keepdims=True)
        take = cnt >= cap
        return jnp.where(take, mid, lo), jnp.where(take, hi, mid)

    lo, _ = lax.fori_loop(0, 31, body, (jnp.zeros((e, 1), jnp.int32), jnp.full((e, 1), ONE_BITS + 1, jnp.int32)))
    thr_ref[...] = jnp.broadcast_to(lo, thr_ref.shape)


def _expert_choice(aff_t, cap):
    e, n = aff_t.shape
    thr = pl.pallas_call(
        functools.partial(_threshold_kernel, cap=cap),
        out_shape=jax.ShapeDtypeStruct((e, HEAD_DIM), jnp.int32),
        compiler_params=pltpu.CompilerParams(vmem_limit_bytes=VMEM_LIMIT),
        name="expert_threshold",
    )(aff_t)[:, :1]
    bits = lax.bitcast_convert_type(aff_t, jnp.int32)
    above = bits > thr
    at = bits == thr
    need = cap - jnp.sum(above, axis=1, keepdims=True)
    picked = above | (at & (jnp.cumsum(at, axis=1) <= need))
    rank = jnp.cumsum(picked.astype(jnp.int32), axis=1)
    slots = jnp.arange(1, cap + 1, dtype=jnp.int32)
    idx = jax.vmap(lambda row: jnp.searchsorted(row, slots, side="left"))(rank).astype(jnp.int32)
    slot = jnp.where(picked, rank - 1, -1)
    first = jnp.concatenate([jnp.zeros((e, 1), jnp.int32), rank[:, CMB_TM - 1:-1:CMB_TM]], axis=1)
    return jnp.take_along_axis(aff_t, idx, axis=1), idx, slot, first


CMB_TM = 128
CMB_RB = 64
CMB_WIN = CMB_TM // CMB_RB + 1


def _combine_kernel(blk_ref, slot_ref, x_ref, g_ref, b_ref, *rest):
    ye_refs, o_ref = rest[:-1], rest[-1]
    i = pl.program_id(0)
    slot = slot_ref[...]
    tm = slot.shape[0]
    span = CMB_WIN * CMB_RB
    lane = lax.broadcasted_iota(jnp.int32, (tm, span), 1)
    one, zero = jnp.ones((tm, span), F32), jnp.zeros((tm, span), F32)
    parts = []
    for e in range(N_EXPERTS):
        local = slot[:, e:e + 1] - blk_ref[i * N_EXPERTS + e] * CMB_RB
        parts.append(jnp.where(lane == local, one, zero).astype(BF16))
    onehot = jnp.concatenate(parts, axis=1)
    rows = jnp.concatenate([r[0] for r in ye_refs], axis=0)
    f = jnp.dot(onehot, rows, preferred_element_type=F32)
    o_ref[...] = _layer_norm(ALPHA * x_ref[...] + f, g_ref[...], b_ref[...])


def _combine_ln(ye, slot, first_blk, x1, ln_g, ln_b):
    n = x1.shape[0]
    e, rows, _ = ye.shape
    last = rows // CMB_RB - 1
    row = pl.BlockSpec((CMB_TM, D_MODEL), lambda i, blk: (i, 0))
    const = pl.BlockSpec((1, D_MODEL), lambda i, blk: (0, 0))

    def ye_spec(ei, k):
        return pl.BlockSpec((1, CMB_RB, D_MODEL),
                            lambda i, blk: (ei, jnp.minimum(blk[i * N_EXPERTS + ei] + k, last), 0))

    grid_spec = pltpu.PrefetchScalarGridSpec(
        num_scalar_prefetch=1,
        grid=(n // CMB_TM,),
        in_specs=[pl.BlockSpec((CMB_TM, N_EXPERTS), lambda i, blk: (i, 0)), row, const, const]
        + [ye_spec(ei, k) for ei in range(e) for k in range(CMB_WIN)],
        out_specs=row,
    )
    return pl.pallas_call(
        _combine_kernel,
        grid_spec=grid_spec,
        out_shape=jax.ShapeDtypeStruct((n, D_MODEL), F32),
        compiler_params=_cparams("arbitrary"),
        name="combine_ln",
    )(first_blk, slot, x1, ln_g.reshape(1, D_MODEL), ln_b.reshape(1, D_MODEL), *([ye] * (e * CMB_WIN)))


FFN_TM = 512
FFN_CHUNK = 256


def _ffn_kernel(x_ref, gate_ref, w1_ref, w3_ref, w2_ref, o_ref):
    x = x_ref[0]
    acc = jnp.zeros((x.shape[0], D_MODEL), F32)
    for c in range(D_FF // FFN_CHUNK):
        cols = slice(c * FFN_CHUNK, (c + 1) * FFN_CHUNK)
        g = jnp.dot(x, w1_ref[0, :, cols], preferred_element_type=F32)
        u = jnp.dot(x, w3_ref[0, :, cols], preferred_element_type=F32)
        h = (g * jax.nn.sigmoid(g) * u).astype(BF16)
        acc = acc + jnp.dot(h, w2_ref[0, cols, :], preferred_element_type=F32)
    o_ref[0] = (acc * gate_ref[0]).astype(BF16)


def _expert_ffn(xe, gates, w1, w3, w2):
    e, rows, _ = xe.shape
    tm = FFN_TM
    return pl.pallas_call(
        _ffn_kernel,
        grid=(e, rows // tm),
        in_specs=[
            pl.BlockSpec((1, tm, D_MODEL), lambda ei, i: (ei, i, 0)),
            pl.BlockSpec((1, tm, 1), lambda ei, i: (ei, i, 0)),
            pl.BlockSpec((1, D_MODEL, D_FF), lambda ei, i: (ei, 0, 0)),
            pl.BlockSpec((1, D_MODEL, D_FF), lambda ei, i: (ei, 0, 0)),
            pl.BlockSpec((1, D_FF, D_MODEL), lambda ei, i: (ei, 0, 0)),
        ],
        out_specs=pl.BlockSpec((1, tm, D_MODEL), lambda ei, i: (ei, i, 0)),
        out_shape=jax.ShapeDtypeStruct((e, rows, D_MODEL), BF16),
        compiler_params=_cparams("arbitrary", "arbitrary"),
        name="expert_ffn",
    )(xe, gates, w1, w3, w2)


def _trunk(x2d, group_rows, batch, seq, w_in, w_out, lambda_q1, lambda_k1, lambda_q2, lambda_k2, subln_g,
           ln1_g, ln1_b, w_router, w_gate, w_up, w_down, ln2_g, ln2_b):
    tables = _rope_tables(seq)
    depth = w_in.shape[0]
    for l in range(depth):
        lambda_init = 0.8 - 0.6 * math.exp(-0.3 * l)
        lam = (jnp.exp(jnp.sum(lambda_q1[l] * lambda_k1[l])) - jnp.exp(jnp.sum(lambda_q2[l] * lambda_k2[l]))
               + lambda_init)
        scalars = jnp.stack([lam, jnp.asarray(1.0 - lambda_init, F32)]).astype(F32)

        *qkv_groups, qkv_b = _in_proj(x2d, _prep_w_in(w_in[l]), tables, batch, seq)
        o_groups, lse_groups = [], []
        for g in range(len(DILATIONS)):
            o, lse = _dilated_group(qkv_groups[g], batch, seq, g)
            o_groups.append(o)
            lse_groups.append(lse)
        o_groups[0] = o_groups[0].reshape(batch * seq, WIDTH_A)
        lse_groups[0] = lse_groups[0].reshape(batch * seq, WIDTH_A)
        mix_b = _diff_attention(qkv_b, scalars, subln_g[l], batch, seq)
        x1, x1b, aff = _out_proj(o_groups, lse_groups, mix_b, x2d, w_out[l].astype(BF16), ln1_g[l], ln1_b[l],
                                 w_router[l], seq)

        xe_parts, gate_parts, slot_parts, blk_parts = [], [], [], []
        start = row_off = 0
        for rows in group_rows:
            cap = CAPACITY_FACTOR * rows // N_EXPERTS
            gates, idx, slot, first = _expert_choice(aff[start:start + rows].T, cap)
            xe_parts.append(x1b[start:start + rows][idx])
            gate_parts.append(gates)
            slot_parts.append(jnp.where(slot >= 0, slot + row_off, -1))
            blk_parts.append((first + row_off) // CMB_RB)
            start += rows
            row_off += cap
        xe = jnp.concatenate(xe_parts, axis=1)
        gates = jnp.concatenate(gate_parts, axis=1)
        slot = jnp.concatenate(slot_parts, axis=1).T
        first_blk = jnp.concatenate(blk_parts, axis=1).T.reshape(-1)
        ye = _expert_ffn(xe, gates[..., None], w_gate[l].astype(BF16), w_up[l].astype(BF16),
                         w_down[l].astype(BF16))
        x2d = _combine_ln(ye, slot, first_blk, x1, ln2_g[l], ln2_b[l])
    return x2d


def kernel(x_prompt, x_sample, w_in, w_out, lambda_q1, lambda_k1, lambda_q2, lambda_k2, subln_g, ln1_g, ln1_b,
           w_router, w_gate, w_up, w_down, ln2_g, ln2_b):
    bp, seq, d = x_prompt.shape
    bs = x_sample.shape[0]
    assert x_sample.shape[1] == seq
    x2d = jnp.concatenate([x_prompt.reshape(bp * seq, d), x_sample.reshape(bs * seq, d)], axis=0)
    y = _trunk(x2d, (bp * seq, bs * seq), bp + bs, seq, w_in, w_out, lambda_q1, lambda_k1, lambda_q2, lambda_k2,
               subln_g, ln1_g, ln1_b, w_router, w_gate, w_up, w_down, ln2_g, ln2_b)
    return y[:bp * seq].reshape(bp, seq, d), y[bp * seq:].reshape(bs, seq, d)
```

```python
import functools
import math

import jax
import jax.numpy as jnp
from jax import lax
from jax.experimental import pallas as pl
from jax.experimental.pallas import tpu as pltpu

D_MODEL = 1024
DEPTH = 4
DILATIONS = (1, 4, 16)
RADIUS = 64
HEADS_A = 4
HEAD_DIM = 128
WIDTH_A = HEADS_A * HEAD_DIM
QKV_A = len(DILATIONS) * WIDTH_A
HEADS_B = 4
SUB_DIM_B = 64
WIDTH_B = HEADS_B * HEAD_DIM
IN_COLS = 3 * QKV_A + 3 * WIDTH_B
N_EXPERTS = 16
D_FF = 2816
CAPACITY_FACTOR = 2
ROPE_THETA = 10000.0
LN_EPS = 1e-5
NEG = -1e30
ALPHA = (2 * DEPTH) ** 0.25
LOG2E = 1.4426950408889634
LN2 = 0.6931471805599453

QA_BLK, KA_BLK, VA_BLK = 0, QKV_A // HEAD_DIM, 2 * QKV_A // HEAD_DIM
QB_BLK = 3 * QKV_A // HEAD_DIM
KB_BLK = QB_BLK + WIDTH_B // HEAD_DIM
VB_BLK = KB_BLK + WIDTH_B // HEAD_DIM
N_BLKS = IN_COLS // HEAD_DIM

VMEM_LIMIT = 56 * 1024 * 1024
BF16 = jnp.bfloat16
F32 = jnp.float32


def _cparams(*sem):
    return pltpu.CompilerParams(dimension_semantics=sem, vmem_limit_bytes=VMEM_LIMIT)


IN_TM = 512
IN_CHUNK = 512


def _in_proj_kernel(x_ref, w_ref, tab_ref, a0_ref, a1_ref, a2_ref, b_ref, scr_ref):
    xb = x_ref[...].astype(BF16)
    tm = xb.shape[0]
    group_refs = (a0_ref, a1_ref, a2_ref)
    n_groups = len(DILATIONS)
    for ch in range(IN_COLS // IN_CHUNK):
        c0 = ch * IN_CHUNK
        acc = jnp.dot(xb, w_ref[:, c0:c0 + IN_CHUNK], preferred_element_type=F32)
        if ch < 3 * n_groups:
            sec, g = divmod(ch, n_groups)
            tab = (0, 2, None)[sec]
        else:
            sec, g = ch - 3 * n_groups, None
            tab = (4, 6, None)[sec]
        blocks = []
        for j in range(IN_CHUNK // HEAD_DIM):
            blk = acc[:, j * HEAD_DIM:(j + 1) * HEAD_DIM]
            if tab is not None:
                blk = blk * tab_ref[tab] + pltpu.roll(blk, HEAD_DIM // 2, 1) * tab_ref[tab + 1]
            blocks.append(blk)
        if g is None or g == 0:
            dst = b_ref if g is None else a0_ref
            for j, blk in enumerate(blocks):
                lo = sec * WIDTH_A + j * HEAD_DIM
                dst[:, lo:lo + HEAD_DIM] = blk.astype(BF16)
        else:
            r = DILATIONS[g]
            for j, blk in enumerate(blocks):
                scr_ref[j] = blk
            for c in range(r):
                for j in range(len(blocks)):
                    rows = scr_ref[j, pl.ds(c, tm // r, stride=r), :]
                    lo = c * 3 * WIDTH_A + sec * WIDTH_A + j * HEAD_DIM
                    group_refs[g][0, :, lo:lo + HEAD_DIM] = rows.astype(BF16)


def _in_proj(x2d, w_bf, tables, batch, seq):
    n = x2d.shape[0]
    tm = min(IN_TM, seq)
    tiles_per_seq = seq // tm
    r1, r2 = DILATIONS[1], DILATIONS[2]
    qkv = 3 * WIDTH_A
    seq_map = lambda i: (i // tiles_per_seq, i % tiles_per_seq, 0)
    return pl.pallas_call(
        _in_proj_kernel,
        grid=(n // tm,),
        in_specs=[
            pl.BlockSpec((tm, D_MODEL), lambda i: (i, 0)),
            pl.BlockSpec((D_MODEL, IN_COLS), lambda i: (0, 0), pipeline_mode=pl.Buffered(1)),
            pl.BlockSpec((8, tm, HEAD_DIM), lambda i: (0, i % tiles_per_seq, 0)),
        ],
        out_specs=[
            pl.BlockSpec((tm, qkv), lambda i: (i, 0)),
            pl.BlockSpec((1, tm // r1, r1 * qkv), seq_map),
            pl.BlockSpec((1, tm // r2, r2 * qkv), seq_map),
            pl.BlockSpec((tm, qkv), lambda i: (i, 0)),
        ],
        out_shape=[
            jax.ShapeDtypeStruct((n, qkv), BF16),
            jax.ShapeDtypeStruct((batch, seq // r1, r1 * qkv), BF16),
            jax.ShapeDtypeStruct((batch, seq // r2, r2 * qkv), BF16),
            jax.ShapeDtypeStruct((n, qkv), BF16),
        ],
        scratch_shapes=[pltpu.VMEM((IN_CHUNK // HEAD_DIM, tm, HEAD_DIM), F32)],
        compiler_params=_cparams("parallel"),
        name="in_proj_rope",
    )(x2d, w_bf, tables)


def _rope_tables(seq):
    pos = jnp.arange(seq, dtype=F32)[:, None]

    def cs(d):
        inv = ROPE_THETA ** (-jnp.arange(0, d, 2, dtype=F32) / d)
        ang = pos * inv[None, :]
        return jnp.cos(ang), jnp.sin(ang)

    ca, sa = cs(HEAD_DIM)
    cos_a = jnp.concatenate([ca, ca], axis=1)
    sin_a = jnp.concatenate([-sa, sa], axis=1)
    cb, sb = cs(SUB_DIM_B)
    cos_b = jnp.concatenate([cb, cb, cb, cb], axis=1)
    sin_b = jnp.concatenate([-sb, -sb, sb, sb], axis=1)
    qa = HEAD_DIM ** -0.5 * LOG2E
    qb = SUB_DIM_B ** -0.5 * LOG2E
    return jnp.stack([cos_a * qa, sin_a * qa, cos_a, sin_a, cos_b * qb, sin_b * qb, cos_b, sin_b], axis=0)


def _prep_w_in(w_in):
    half = SUB_DIM_B // 2
    perm = []
    for quarter in range(4):
        t, hi = quarter % 2, quarter // 2
        perm.extend(t * SUB_DIM_B + hi * half + i for i in range(half))
    perm = jnp.asarray(perm, dtype=jnp.int32)
    cols = jnp.arange(IN_COLS, dtype=jnp.int32)
    start, stop = QB_BLK * HEAD_DIM, VB_BLK * HEAD_DIM
    inside = (cols >= start) & (cols < stop)
    rel = cols - start
    permuted = start + (rel // HEAD_DIM) * HEAD_DIM + perm[rel % HEAD_DIM]
    src = jnp.where(inside, permuted, cols)
    return jnp.take(w_in, src, axis=1).astype(BF16)


A_QB = 128
A_WIN = A_QB + 2 * RADIUS
A_UNROLL = 8


def _dilated_kernel(q_ref, k_ref, v_ref, o_ref, lse_ref, *, sub_len, heads):
    nq = sub_len // A_QB
    row = lax.broadcasted_iota(jnp.int32, (A_QB, A_WIN), 0)
    col = lax.broadcasted_iota(jnp.int32, (A_QB, A_WIN), 1)
    rel0 = col - row

    for h in range(heads):
        lanes = slice(h * HEAD_DIM, (h + 1) * HEAD_DIM)

        def body(qi, carry, lanes=lanes):
            q0 = pl.multiple_of(qi * A_QB, A_QB)
            ws = pl.multiple_of(jnp.clip(q0 - RADIUS, 0, sub_len - A_WIN), RADIUS)
            qb = q_ref[0, pl.ds(q0, A_QB), lanes]
            kw = k_ref[0, pl.ds(ws, A_WIN), lanes]
            vw = v_ref[0, pl.ds(ws, A_WIN), lanes]
            s = lax.dot_general(qb, kw, (((1,), (1,)), ((), ())), preferred_element_type=F32)
            rel = rel0 - (q0 - ws)
            s = jnp.where(jnp.abs(rel) <= RADIUS, s, NEG)
            m = jnp.max(s, axis=1, keepdims=True)
            p = jnp.exp2(s - m)
            l = jnp.sum(p, axis=1, keepdims=True)
            o = jnp.dot(p.astype(BF16), vw, preferred_element_type=F32)
            o_ref[0, pl.ds(q0, A_QB), lanes] = o / l
            lse = (m + jnp.log2(l)) * LN2
            lse_ref[0, pl.ds(q0, A_QB), lanes] = jnp.broadcast_to(lse, (A_QB, HEAD_DIM))
            return carry

        lax.fori_loop(0, nq, body, 0, unroll=min(nq, A_UNROLL))


def _dilated_group(qkv, batch, seq, g):
    r = DILATIONS[g]
    sub_len = seq // r
    assert sub_len % A_QB == 0 and sub_len >= A_WIN
    heads = 1 if r == 1 else HEADS_A
    width = heads * HEAD_DIM
    steps = HEADS_A // heads
    per_sec = WIDTH_A // width
    view = qkv.reshape(batch, sub_len, r * 3 * WIDTH_A)

    def in_map(sec):
        return lambda b, c, hh: (b, 0, (c * 3 + sec) * per_sec + hh)

    out_map = lambda b, c, hh: (b, 0, c * steps + hh)
    blk = (1, sub_len, width)
    return pl.pallas_call(
        functools.partial(_dilated_kernel, sub_len=sub_len, heads=heads),
        grid=(batch, r, steps),
        in_specs=[pl.BlockSpec(blk, in_map(0)), pl.BlockSpec(blk, in_map(1)), pl.BlockSpec(blk, in_map(2))],
        out_specs=[pl.BlockSpec(blk, out_map), pl.BlockSpec(blk, out_map)],
        out_shape=[jax.ShapeDtypeStruct((batch, sub_len, r * WIDTH_A), F32)] * 2,
        compiler_params=_cparams("parallel", "parallel", "parallel"),
        name=f"dilated_attn_r{r}",
    )(view, view, view)


B_TQ = 512
B_TK = 512
B_UNROLL = 4


def _diff_kernel(scal_ref, q_ref, k_ref, vt_ref, g_ref, o_ref, *s_scrs, seq):
    lam = scal_ref[0]
    out_scale = scal_ref[1]
    q = q_ref[0]
    tq = q.shape[0]
    lane = lax.broadcasted_iota(jnp.int32, q.shape, 1)
    first = ((lane // (SUB_DIM_B // 2)) % 2) == 0
    zero = jnp.zeros_like(q)
    nk = seq // B_TK
    qs = (jnp.where(first, q, zero), jnp.where(first, zero, q))

    def scores(j, t, m):
        k0 = pl.multiple_of(j * B_TK, B_TK)
        s = lax.dot_general(k_ref[0, pl.ds(k0, B_TK), :], qs[t], (((1,), (1,)), ((), ())),
                            preferred_element_type=F32)
        s_scrs[t][pl.ds(k0, B_TK), :] = s
        return jnp.maximum(m, jnp.max(s, axis=0, keepdims=True))

    def weigh(j, t, m, l, acc):
        k0 = pl.multiple_of(j * B_TK, B_TK)
        p = jnp.exp2(s_scrs[t][pl.ds(k0, B_TK), :] - m)
        l = l + jnp.sum(p, axis=0, keepdims=True)
        acc = acc + jnp.dot(vt_ref[0, 0, :, pl.ds(k0, B_TK)], p.astype(BF16), preferred_element_type=F32)
        return l, acc

    m_init = jnp.full((1, tq), -jnp.inf, F32)
    la_init = (jnp.zeros((1, tq), F32), jnp.zeros((HEAD_DIM, tq), F32))
    m0 = lax.fori_loop(0, nk, lambda j, m: scores(j, 0, m), m_init, unroll=B_UNROLL)

    def middle(j, carry):
        m1, l0, acc0 = carry
        l0, acc0 = weigh(j, 0, m0, l0, acc0)
        return scores(j, 1, m1), l0, acc0

    m1, l0, acc0 = lax.fori_loop(0, nk, middle, (m_init,) + la_init, unroll=B_UNROLL)
    l1, acc1 = lax.fori_loop(0, nk, lambda j, c: weigh(j, 1, m1, *c), la_init, unroll=B_UNROLL)
    o = acc0 / l0 - lam * (acc1 / l1)
    o = o * lax.rsqrt(jnp.mean(o * o, axis=0, keepdims=True) + LN_EPS) * g_ref[...]
    o_ref[0] = (o * out_scale).T.astype(BF16)


def _diff_attention(qkv_b, scalars, subln_g, batch, seq):
    view = qkv_b.reshape(batch, seq, 3 * WIDTH_B)
    v_t = view[:, :, 2 * WIDTH_B:].reshape(batch, seq, HEADS_B, HEAD_DIM).transpose(0, 2, 3, 1)
    tq = min(B_TQ, seq)
    out = pl.pallas_call(
        functools.partial(_diff_kernel, seq=seq),
        grid=(batch, HEADS_B, seq // tq),
        in_specs=[
            pl.BlockSpec(memory_space=pltpu.SMEM),
            pl.BlockSpec((1, tq, HEAD_DIM), lambda b, h, i: (b, i, h)),
            pl.BlockSpec((1, seq, HEAD_DIM), lambda b, h, i: (b, 0, HEADS_B + h)),
            pl.BlockSpec((1, 1, HEAD_DIM, seq), lambda b, h, i: (b, h, 0, 0)),
            pl.BlockSpec((HEAD_DIM, 1), lambda b, h, i: (0, 0)),
        ],
        out_specs=pl.BlockSpec((1, tq, HEAD_DIM), lambda b, h, i: (b, i, h)),
        out_shape=jax.ShapeDtypeStruct((batch, seq, WIDTH_B), BF16),
        scratch_shapes=[pltpu.VMEM((seq, tq), F32)] * 2,
        compiler_params=_cparams("parallel", "parallel", "arbitrary"),
        name="diff_attn",
    )(scalars, view, view, v_t, subln_g.reshape(HEAD_DIM, 1).astype(F32))
    return out.reshape(batch * seq, WIDTH_B)


OUT_TM = 256


def _layer_norm(y, g, b):
    mu = jnp.mean(y, axis=1, keepdims=True)
    d = y - mu
    var = jnp.mean(d * d, axis=1, keepdims=True)
    return d * lax.rsqrt(var + LN_EPS) * g + b


def _token_order(ref, r, head, scr):
    rows = scr.shape[0]
    for c in range(r):
        lo = c * WIDTH_A + head * HEAD_DIM
        scr[pl.ds(c, rows // r, stride=r), :] = ref[0, :, lo:lo + HEAD_DIM]
    return scr[...]


def _out_proj_kernel(o0, o1, o2, l0, l1, l2, mb_ref, x_ref, w_ref, g_ref, b_ref, wr_ref,
                     x1_ref, x1b_ref, aff_ref, so1, so2, sl1, sl2):
    r1, r2 = DILATIONS[1], DILATIONS[2]
    heads = []
    for h in range(HEADS_A):
        lanes = slice(h * HEAD_DIM, (h + 1) * HEAD_DIM)
        la, lb, lc = l0[:, lanes], _token_order(l1, r1, h, sl1), _token_order(l2, r2, h, sl2)
        mx = jnp.maximum(jnp.maximum(la, lb), lc)
        ea, eb, ec = jnp.exp(la - mx), jnp.exp(lb - mx), jnp.exp(lc - mx)
        num = ea * o0[:, lanes] + eb * _token_order(o1, r1, h, so1) + ec * _token_order(o2, r2, h, so2)
        heads.append((num / (ea + eb + ec)).astype(BF16))
    mix_a = jnp.concatenate(heads, axis=1)
    m = jnp.dot(mix_a, w_ref[:WIDTH_A, :], preferred_element_type=F32)
    m = m + jnp.dot(mb_ref[...], w_ref[WIDTH_A:, :], preferred_element_type=F32)
    x1 = _layer_norm(ALPHA * x_ref[...] + m, g_ref[...], b_ref[...])
    x1_ref[...] = x1
    x1b_ref[...] = x1.astype(BF16)
    logits = jnp.dot(x1, wr_ref[...], preferred_element_type=F32, precision=lax.Precision.HIGHEST)
    e = jnp.exp(logits - jnp.max(logits, axis=1, keepdims=True))
    aff_ref[...] = e / jnp.sum(e, axis=1, keepdims=True)


def _out_proj(o_groups, lse_groups, mix_b, x2d, w_out_bf, ln_g, ln_b, w_router, seq):
    n = x2d.shape[0]
    tm = OUT_TM
    tiles_per_seq = seq // tm
    r1, r2 = DILATIONS[1], DILATIONS[2]
    row_a = pl.BlockSpec((tm, WIDTH_A), lambda i: (i, 0))
    row_d = pl.BlockSpec((tm, D_MODEL), lambda i: (i, 0))
    seq_map = lambda i: (i // tiles_per_seq, i % tiles_per_seq, 0)
    res1 = pl.BlockSpec((1, tm // r1, r1 * WIDTH_A), seq_map)
    res2 = pl.BlockSpec((1, tm // r2, r2 * WIDTH_A), seq_map)
    const = lambda shape: pl.BlockSpec(shape, lambda i: (0, 0))
    return pl.pallas_call(
        _out_proj_kernel,
        grid=(n // tm,),
        in_specs=[row_a, res1, res2, row_a, res1, res2, row_a, row_d, const((D_MODEL, D_MODEL)),
                  const((1, D_MODEL)), const((1, D_MODEL)), const((D_MODEL, N_EXPERTS))],
        out_specs=[row_d, row_d, pl.BlockSpec((tm, N_EXPERTS), lambda i: (i, 0))],
        out_shape=[jax.ShapeDtypeStruct((n, D_MODEL), F32), jax.ShapeDtypeStruct((n, D_MODEL), BF16),
                   jax.ShapeDtypeStruct((n, N_EXPERTS), F32)],
        scratch_shapes=[pltpu.VMEM((tm, HEAD_DIM), F32)] * 4,
        compiler_params=_cparams("parallel"),
        name="out_proj_ln_router",
    )(*o_groups, *lse_groups, mix_b, x2d, w_out_bf, ln_g.reshape(1, D_MODEL), ln_b.reshape(1, D_MODEL), w_router)


ONE_BITS = 0x3F800000


def _threshold_kernel(aff_ref, thr_ref, *, cap):
    bits = pltpu.bitcast(aff_ref[...], jnp.int32)
    e = bits.shape[0]

    def body(_, carry):
        lo, hi = carry
        mid = lo + (hi - lo) // 2
        cnt = jnp.sum((bits >= mid).astype(jnp.int32), axis=1, keepdims=True)
        take = cnt >= cap
        return jnp.where(take, mid, lo), jnp.where(take, hi, mid)

    lo, _ = lax.fori_loop(0, 31, body, (jnp.zeros((e, 1), jnp.int32), jnp.full((e, 1), ONE_BITS + 1, jnp.int32)))
    thr_ref[...] = jnp.broadcast_to(lo, thr_ref.shape)


def _expert_choice(aff_t, cap):
    e, n = aff_t.shape
    thr = pl.pallas_call(
        functools.partial(_threshold_kernel, cap=cap),
        out_shape=jax.ShapeDtypeStruct((e, HEAD_DIM), jnp.int32),
        compiler_params=pltpu.CompilerParams(vmem_limit_bytes=VMEM_LIMIT),
        name="expert_threshold",
    )(aff_t)[:, :1]
    bits = lax.bitcast_convert_type(aff_t, jnp.int32)
    above = bits > thr
    at = bits == thr
    need = cap - jnp.sum(above, axis=1, keepdims=True)
    picked = above | (at & (jnp.cumsum(at, axis=1) <= need))
    rank = jnp.cumsum(picked.astype(jnp.int32), axis=1)
    slots = jnp.arange(1, cap + 1, dtype=jnp.int32)
    idx = jax.vmap(lambda row: jnp.searchsorted(row, slots, side="left"))(rank).astype(jnp.int32)
    slot = jnp.where(picked, rank - 1, -1)
    first = jnp.concatenate([jnp.zeros((e, 1), jnp.int32), rank[:, CMB_TM - 1:-1:CMB_TM]], axis=1)
    return jnp.take_along_axis(aff_t, idx, axis=1), idx, slot, first


CMB_TM = 128
CMB_RB = 64
CMB_WIN = CMB_TM // CMB_RB + 1


def _combine_kernel(blk_ref, slot_ref, x_ref, g_ref, b_ref, *rest):
    ye_refs, o_ref = rest[:-1], rest[-1]
    i = pl.program_id(0)
    slot = slot_ref[...]
    tm = slot.shape[0]
    span = CMB_WIN * CMB_RB
    lane = lax.broadcasted_iota(jnp.int32, (tm, span), 1)
    one, zero = jnp.ones((tm, span), F32), jnp.zeros((tm, span), F32)
    parts = []
    for e in range(N_EXPERTS):
        local = slot[:, e:e + 1] - blk_ref[i * N_EXPERTS + e] * CMB_RB
        parts.append(jnp.where(lane == local, one, zero).astype(BF16))
    onehot = jnp.concatenate(parts, axis=1)
    rows = jnp.concatenate([r[0] for r in ye_refs], axis=0)
    f = jnp.dot(onehot, rows, preferred_element_type=F32)
    o_ref[...] = _layer_norm(ALPHA * x_ref[...] + f, g_ref[...], b_ref[...])


def _combine_ln(ye, slot, first_blk, x1, ln_g, ln_b):
    n = x1.shape[0]
    e, rows, _ = ye.shape
    last = rows // CMB_RB - 1
    row = pl.BlockSpec((CMB_TM, D_MODEL), lambda i, blk: (i, 0))
    const = pl.BlockSpec((1, D_MODEL), lambda i, blk: (0, 0))

    def ye_spec(ei, k):
        return pl.BlockSpec((1, CMB_RB, D_MODEL),
                            lambda i, blk: (ei, jnp.minimum(blk[i * N_EXPERTS + ei] + k, last), 0))

    grid_spec = pltpu.PrefetchScalarGridSpec(
        num_scalar_prefetch=1,
        grid=(n // CMB_TM,),
        in_specs=[pl.BlockSpec((CMB_TM, N_EXPERTS), lambda i, blk: (i, 0)), row, const, const]
        + [ye_spec(ei, k) for ei in range(e) for k in range(CMB_WIN)],
        out_specs=row,
    )
    return pl.pallas_call(
        _combine_kernel,
        grid_spec=grid_spec,
        out_shape=jax.ShapeDtypeStruct((n, D_MODEL), F32),
        compiler_params=_cparams("arbitrary"),
        name="combine_ln",
    )(first_blk, slot, x1, ln_g.reshape(1, D_MODEL), ln_b.reshape(1, D_MODEL), *([ye] * (e * CMB_WIN)))


FFN_TM = 512
FFN_CHUNK = 256
CAST_ROWS = 512


def _cast_kernel(w_ref, o_ref):
    o_ref[...] = w_ref[...].astype(BF16)


def _layer_weights_bf16(w, l):
    _, e, a, b = w.shape
    rows = math.gcd(a, CAST_ROWS)
    return pl.pallas_call(
        _cast_kernel,
        grid=(e, a // rows),
        in_specs=[pl.BlockSpec((None, None, rows, b), lambda ei, i: (l, ei, i, 0))],
        out_specs=pl.BlockSpec((None, rows, b), lambda ei, i: (ei, i, 0)),
        out_shape=jax.ShapeDtypeStruct((e, a, b), BF16),
        compiler_params=_cparams("parallel", "parallel"),
        name="weights_to_bf16",
    )(w)


def _ffn_kernel(x_ref, gate_ref, w1_ref, w3_ref, w2_ref, o_ref):
    x = x_ref[0]
    acc = jnp.zeros((x.shape[0], D_MODEL), F32)
    for c in range(D_FF // FFN_CHUNK):
        cols = slice(c * FFN_CHUNK, (c + 1) * FFN_CHUNK)
        g = jnp.dot(x, w1_ref[0, :, cols], preferred_element_type=F32)
        u = jnp.dot(x, w3_ref[0, :, cols], preferred_element_type=F32)
        h = (g * jax.nn.sigmoid(g) * u).astype(BF16)
        acc = acc + jnp.dot(h, w2_ref[0, cols, :], preferred_element_type=F32)
    o_ref[0] = (acc * gate_ref[0]).astype(BF16)


def _expert_ffn(xe, gates, w1, w3, w2):
    e, rows, _ = xe.shape
    tm = FFN_TM
    return pl.pallas_call(
        _ffn_kernel,
        grid=(e, rows // tm),
        in_specs=[
            pl.BlockSpec((1, tm, D_MODEL), lambda ei, i: (ei, i, 0)),
            pl.BlockSpec((1, tm, 1), lambda ei, i: (ei, i, 0)),
            pl.BlockSpec((1, D_MODEL, D_FF), lambda ei, i: (ei, 0, 0)),
            pl.BlockSpec((1, D_MODEL, D_FF), lambda ei, i: (ei, 0, 0)),
            pl.BlockSpec((1, D_FF, D_MODEL), lambda ei, i: (ei, 0, 0)),
        ],
        out_specs=pl.BlockSpec((1, tm, D_MODEL), lambda ei, i: (ei, i, 0)),
        out_shape=jax.ShapeDtypeStruct((e, rows, D_MODEL), BF16),
        compiler_params=_cparams("arbitrary", "arbitrary"),
        name="expert_ffn",
    )(xe, gates, w1, w3, w2)


def _trunk(x2d, group_rows, batch, seq, w_in, w_out, lambda_q1, lambda_k1, lambda_q2, lambda_k2, subln_g,
           ln1_g, ln1_b, w_router, w_gate, w_up, w_down, ln2_g, ln2_b):
    tables = _rope_tables(seq)
    depth = w_in.shape[0]
    for l in range(depth):
        lambda_init = 0.8 - 0.6 * math.exp(-0.3 * l)
        lam = (jnp.exp(jnp.sum(lambda_q1[l] * lambda_k1[l])) - jnp.exp(jnp.sum(lambda_q2[l] * lambda_k2[l]))
               + lambda_init)
        scalars = jnp.stack([lam, jnp.asarray(1.0 - lambda_init, F32)]).astype(F32)

        *qkv_groups, qkv_b = _in_proj(x2d, _prep_w_in(w_in[l]), tables, batch, seq)
        o_groups, lse_groups = [], []
        for g in range(len(DILATIONS)):
            o, lse = _dilated_group(qkv_groups[g], batch, seq, g)
            o_groups.append(o)
            lse_groups.append(lse)
        o_groups[0] = o_groups[0].reshape(batch * seq, WIDTH_A)
        lse_groups[0] = lse_groups[0].reshape(batch * seq, WIDTH_A)
        mix_b = _diff_attention(qkv_b, scalars, subln_g[l], batch, seq)
        x1, x1b, aff = _out_proj(o_groups, lse_groups, mix_b, x2d, w_out[l].astype(BF16), ln1_g[l], ln1_b[l],
                                 w_router[l], seq)

        xe_parts, gate_parts, slot_parts, blk_parts = [], [], [], []
        start = row_off = 0
        for rows in group_rows:
            cap = CAPACITY_FACTOR * rows // N_EXPERTS
            gates, idx, slot, first = _expert_choice(aff[start:start + rows].T, cap)
            xe_parts.append(idx + start)
            gate_parts.append(gates)
            slot_parts.append(jnp.where(slot >= 0, slot + row_off, -1))
            blk_parts.append((first + row_off) // CMB_RB)
            start += rows
            row_off += cap
        xe = x1b[jnp.concatenate(xe_parts, axis=1)]
        gates = jnp.concatenate(gate_parts, axis=1)
        slot = jnp.concatenate(slot_parts, axis=1).T
        first_blk = jnp.concatenate(blk_parts, axis=1).T.reshape(-1)
        ye = _expert_ffn(xe, gates[..., None], _layer_weights_bf16(w_gate, l), _layer_weights_bf16(w_up, l),
                         _layer_weights_bf16(w_down, l))
        x2d = _combine_ln(ye, slot, first_blk, x1, ln2_g[l], ln2_b[l])
    return x2d


def kernel(x_prompt, x_sample, w_in, w_out, lambda_q1, lambda_k1, lambda_q2, lambda_k2, subln_g, ln1_g, ln1_b,
           w_router, w_gate, w_up, w_down, ln2_g, ln2_b):
    bp, seq, d = x_prompt.shape
    bs = x_sample.shape[0]
    assert x_sample.shape[1] == seq
    x2d = jnp.concatenate([x_prompt.reshape(bp * seq, d), x_sample.reshape(bs * seq, d)], axis=0)
    y = _trunk(x2d, (bp * seq, bs * seq), bp + bs, seq, w_in, w_out, lambda_q1, lambda_k1, lambda_q2, lambda_k2,
               subln_g, ln1_g, ln1_b, w_router, w_gate, w_up, w_down, ln2_g, ln2_b)
    return y[:bp * seq].reshape(bp, seq, d), y[bp * seq:].reshape(bs, seq, d)
```

```python
import functools
import math

import jax
import jax.numpy as jnp
from jax import lax
from jax.experimental import pallas as pl
from jax.experimental.pallas import tpu as pltpu

D_MODEL = 1024
DEPTH = 4
DILATIONS = (1, 4, 16)
RADIUS = 64
HEADS_A = 4
HEAD_DIM = 128
WIDTH_A = HEADS_A * HEAD_DIM
QKV_A = len(DILATIONS) * WIDTH_A
HEADS_B = 4
SUB_DIM_B = 64
WIDTH_B = HEADS_B * HEAD_DIM
IN_COLS = 3 * QKV_A + 3 * WIDTH_B
N_EXPERTS = 16
D_FF = 2816
CAPACITY_FACTOR = 2
ROPE_THETA = 10000.0
LN_EPS = 1e-5
NEG = -1e30
ALPHA = (2 * DEPTH) ** 0.25
LOG2E = 1.4426950408889634
LN2 = 0.6931471805599453

QA_BLK, KA_BLK, VA_BLK = 0, QKV_A // HEAD_DIM, 2 * QKV_A // HEAD_DIM
QB_BLK = 3 * QKV_A // HEAD_DIM
KB_BLK = QB_BLK + WIDTH_B // HEAD_DIM
VB_BLK = KB_BLK + WIDTH_B // HEAD_DIM
N_BLKS = IN_COLS // HEAD_DIM

VMEM_LIMIT = 56 * 1024 * 1024
BF16 = jnp.bfloat16
F32 = jnp.float32


def _cparams(*sem):
    return pltpu.CompilerParams(dimension_semantics=sem, vmem_limit_bytes=VMEM_LIMIT)


IN_TM = 512
IN_CHUNK = 512


def _in_proj_kernel(x_ref, w_ref, tab_ref, a0_ref, a1_ref, a2_ref, b_ref, scr_ref):
    xb = x_ref[...].astype(BF16)
    tm = xb.shape[0]
    group_refs = (a0_ref, a1_ref, a2_ref)
    n_groups = len(DILATIONS)
    for ch in range(IN_COLS // IN_CHUNK):
        c0 = ch * IN_CHUNK
        acc = jnp.dot(xb, w_ref[:, c0:c0 + IN_CHUNK], preferred_element_type=F32)
        if ch < 3 * n_groups:
            sec, g = divmod(ch, n_groups)
            tab = (0, 2, None)[sec]
        else:
            sec, g = ch - 3 * n_groups, None
            tab = (4, 6, None)[sec]
        blocks = []
        for j in range(IN_CHUNK // HEAD_DIM):
            blk = acc[:, j * HEAD_DIM:(j + 1) * HEAD_DIM]
            if tab is not None:
                blk = blk * tab_ref[tab] + pltpu.roll(blk, HEAD_DIM // 2, 1) * tab_ref[tab + 1]
            blocks.append(blk)
        if g is None or g == 0:
            dst = b_ref if g is None else a0_ref
            for j, blk in enumerate(blocks):
                lo = sec * WIDTH_A + j * HEAD_DIM
                dst[:, lo:lo + HEAD_DIM] = blk.astype(BF16)
        else:
            r = DILATIONS[g]
            for j, blk in enumerate(blocks):
                scr_ref[j] = blk
            for c in range(r):
                for j in range(len(blocks)):
                    rows = scr_ref[j, pl.ds(c, tm // r, stride=r), :]
                    lo = c * 3 * WIDTH_A + sec * WIDTH_A + j * HEAD_DIM
                    group_refs[g][0, :, lo:lo + HEAD_DIM] = rows.astype(BF16)


def _in_proj(x2d, w_bf, tables, batch, seq):
    n = x2d.shape[0]
    tm = min(IN_TM, seq)
    tiles_per_seq = seq // tm
    r1, r2 = DILATIONS[1], DILATIONS[2]
    qkv = 3 * WIDTH_A
    seq_map = lambda i: (i // tiles_per_seq, i % tiles_per_seq, 0)
    return pl.pallas_call(
        _in_proj_kernel,
        grid=(n // tm,),
        in_specs=[
            pl.BlockSpec((tm, D_MODEL), lambda i: (i, 0)),
            pl.BlockSpec((D_MODEL, IN_COLS), lambda i: (0, 0), pipeline_mode=pl.Buffered(1)),
            pl.BlockSpec((8, tm, HEAD_DIM), lambda i: (0, i % tiles_per_seq, 0)),
        ],
        out_specs=[
            pl.BlockSpec((tm, qkv), lambda i: (i, 0)),
            pl.BlockSpec((1, tm // r1, r1 * qkv), seq_map),
            pl.BlockSpec((1, tm // r2, r2 * qkv), seq_map),
            pl.BlockSpec((tm, qkv), lambda i: (i, 0)),
        ],
        out_shape=[
            jax.ShapeDtypeStruct((n, qkv), BF16),
            jax.ShapeDtypeStruct((batch, seq // r1, r1 * qkv), BF16),
            jax.ShapeDtypeStruct((batch, seq // r2, r2 * qkv), BF16),
            jax.ShapeDtypeStruct((n, qkv), BF16),
        ],
        scratch_shapes=[pltpu.VMEM((IN_CHUNK // HEAD_DIM, tm, HEAD_DIM), F32)],
        compiler_params=_cparams("parallel"),
        name="in_proj_rope",
    )(x2d, w_bf, tables)


def _rope_tables(seq):
    pos = jnp.arange(seq, dtype=F32)[:, None]

    def cs(d):
        inv = ROPE_THETA ** (-jnp.arange(0, d, 2, dtype=F32) / d)
        ang = pos * inv[None, :]
        return jnp.cos(ang), jnp.sin(ang)

    ca, sa = cs(HEAD_DIM)
    cos_a = jnp.concatenate([ca, ca], axis=1)
    sin_a = jnp.concatenate([-sa, sa], axis=1)
    cb, sb = cs(SUB_DIM_B)
    cos_b = jnp.concatenate([cb, cb, cb, cb], axis=1)
    sin_b = jnp.concatenate([-sb, -sb, sb, sb], axis=1)
    qa = HEAD_DIM ** -0.5 * LOG2E
    qb = SUB_DIM_B ** -0.5 * LOG2E
    return jnp.stack([cos_a * qa, sin_a * qa, cos_a, sin_a, cos_b * qb, sin_b * qb, cos_b, sin_b], axis=0)


def _prep_w_in(w_in):
    half = SUB_DIM_B // 2
    perm = []
    for quarter in range(4):
        t, hi = quarter % 2, quarter // 2
        perm.extend(t * SUB_DIM_B + hi * half + i for i in range(half))
    perm = jnp.asarray(perm, dtype=jnp.int32)
    cols = jnp.arange(IN_COLS, dtype=jnp.int32)
    start, stop = QB_BLK * HEAD_DIM, VB_BLK * HEAD_DIM
    inside = (cols >= start) & (cols < stop)
    rel = cols - start
    permuted = start + (rel // HEAD_DIM) * HEAD_DIM + perm[rel % HEAD_DIM]
    src = jnp.where(inside, permuted, cols)
    return jnp.take(w_in, src, axis=1).astype(BF16)


A_QB = 128
A_WIN = A_QB + 2 * RADIUS
A_UNROLL = 8


def _dilated_kernel(q_ref, k_ref, v_ref, o_ref, lse_ref, *, sub_len, heads):
    nq = sub_len // A_QB
    row = lax.broadcasted_iota(jnp.int32, (A_QB, A_WIN), 0)
    col = lax.broadcasted_iota(jnp.int32, (A_QB, A_WIN), 1)
    rel0 = col - row

    for h in range(heads):
        lanes = slice(h * HEAD_DIM, (h + 1) * HEAD_DIM)

        def body(qi, carry, lanes=lanes):
            q0 = pl.multiple_of(qi * A_QB, A_QB)
            ws = pl.multiple_of(jnp.clip(q0 - RADIUS, 0, sub_len - A_WIN), RADIUS)
            qb = q_ref[0, pl.ds(q0, A_QB), lanes]
            kw = k_ref[0, pl.ds(ws, A_WIN), lanes]
            vw = v_ref[0, pl.ds(ws, A_WIN), lanes]
            s = lax.dot_general(qb, kw, (((1,), (1,)), ((), ())), preferred_element_type=F32)
            rel = rel0 - (q0 - ws)
            s = jnp.where(jnp.abs(rel) <= RADIUS, s, NEG)
            m = jnp.max(s, axis=1, keepdims=True)
            p = jnp.exp2(s - m)
            l = jnp.sum(p, axis=1, keepdims=True)
            o = jnp.dot(p.astype(BF16), vw, preferred_element_type=F32)
            o_ref[0, pl.ds(q0, A_QB), lanes] = o / l
            lse = (m + jnp.log2(l)) * LN2
            lse_ref[0, pl.ds(q0, A_QB), lanes] = jnp.broadcast_to(lse, (A_QB, HEAD_DIM))
            return carry

        lax.fori_loop(0, nq, body, 0, unroll=min(nq, A_UNROLL))


def _dilated_group(qkv, batch, seq, g):
    r = DILATIONS[g]
    sub_len = seq // r
    assert sub_len % A_QB == 0 and sub_len >= A_WIN
    heads = 1 if r == 1 else HEADS_A
    width = heads * HEAD_DIM
    steps = HEADS_A // heads
    per_sec = WIDTH_A // width
    view = qkv.reshape(batch, sub_len, r * 3 * WIDTH_A)

    def in_map(sec):
        return lambda b, c, hh: (b, 0, (c * 3 + sec) * per_sec + hh)

    out_map = lambda b, c, hh: (b, 0, c * steps + hh)
    blk = (1, sub_len, width)
    return pl.pallas_call(
        functools.partial(_dilated_kernel, sub_len=sub_len, heads=heads),
        grid=(batch, r, steps),
        in_specs=[pl.BlockSpec(blk, in_map(0)), pl.BlockSpec(blk, in_map(1)), pl.BlockSpec(blk, in_map(2))],
        out_specs=[pl.BlockSpec(blk, out_map), pl.BlockSpec(blk, out_map)],
        out_shape=[jax.ShapeDtypeStruct((batch, sub_len, r * WIDTH_A), F32)] * 2,
        compiler_params=_cparams("parallel", "parallel", "parallel"),
        name=f"dilated_attn_r{r}",
    )(view, view, view)


B_TQ = 1024
B_TK = 512
B_UNROLL = 4


def _diff_kernel(scal_ref, q_ref, k_ref, vt_ref, g_ref, o_ref, *s_scrs, seq):
    lam = scal_ref[0]
    out_scale = scal_ref[1]
    q = q_ref[0]
    tq = q.shape[0]
    lane = lax.broadcasted_iota(jnp.int32, q.shape, 1)
    first = ((lane // (SUB_DIM_B // 2)) % 2) == 0
    zero = jnp.zeros_like(q)
    nk = seq // B_TK
    qs = (jnp.where(first, q, zero), jnp.where(first, zero, q))

    def scores(j, t, m):
        k0 = pl.multiple_of(j * B_TK, B_TK)
        s = lax.dot_general(k_ref[0, pl.ds(k0, B_TK), :], qs[t], (((1,), (1,)), ((), ())),
                            preferred_element_type=F32)
        s_scrs[t][pl.ds(k0, B_TK), :] = s
        return jnp.maximum(m, jnp.max(s, axis=0, keepdims=True))

    def weigh(j, t, m, l, acc):
        k0 = pl.multiple_of(j * B_TK, B_TK)
        p = jnp.exp2(s_scrs[t][pl.ds(k0, B_TK), :] - m)
        l = l + jnp.sum(p, axis=0, keepdims=True)
        acc = acc + jnp.dot(vt_ref[0, 0, :, pl.ds(k0, B_TK)], p.astype(BF16), preferred_element_type=F32)
        return l, acc

    m_init = jnp.full((1, tq), -jnp.inf, F32)
    la_init = (jnp.zeros((1, tq), F32), jnp.zeros((HEAD_DIM, tq), F32))
    m0 = lax.fori_loop(0, nk, lambda j, m: scores(j, 0, m), m_init, unroll=B_UNROLL)

    def middle(j, carry):
        m1, l0, acc0 = carry
        l0, acc0 = weigh(j, 0, m0, l0, acc0)
        return scores(j, 1, m1), l0, acc0

    m1, l0, acc0 = lax.fori_loop(0, nk, middle, (m_init,) + la_init, unroll=B_UNROLL)
    l1, acc1 = lax.fori_loop(0, nk, lambda j, c: weigh(j, 1, m1, *c), la_init, unroll=B_UNROLL)
    o = acc0 / l0 - lam * (acc1 / l1)
    o = o * lax.rsqrt(jnp.mean(o * o, axis=0, keepdims=True) + LN_EPS) * g_ref[...]
    o_ref[0] = (o * out_scale).T.astype(BF16)


def _diff_attention(qkv_b, scalars, subln_g, batch, seq):
    view = qkv_b.reshape(batch, seq, 3 * WIDTH_B)
    v_t = view[:, :, 2 * WIDTH_B:].reshape(batch, seq, HEADS_B, HEAD_DIM).transpose(0, 2, 3, 1)
    tq = min(B_TQ, seq)
    out = pl.pallas_call(
        functools.partial(_diff_kernel, seq=seq),
        grid=(batch, HEADS_B, seq // tq),
        in_specs=[
            pl.BlockSpec(memory_space=pltpu.SMEM),
            pl.BlockSpec((1, tq, HEAD_DIM), lambda b, h, i: (b, i, h)),
            pl.BlockSpec((1, seq, HEAD_DIM), lambda b, h, i: (b, 0, HEADS_B + h)),
            pl.BlockSpec((1, 1, HEAD_DIM, seq), lambda b, h, i: (b, h, 0, 0)),
            pl.BlockSpec((HEAD_DIM, 1), lambda b, h, i: (0, 0)),
        ],
        out_specs=pl.BlockSpec((1, tq, HEAD_DIM), lambda b, h, i: (b, i, h)),
        out_shape=jax.ShapeDtypeStruct((batch, seq, WIDTH_B), BF16),
        scratch_shapes=[pltpu.VMEM((seq, tq), F32)] * 2,
        compiler_params=_cparams("parallel", "parallel", "arbitrary"),
        name="diff_attn",
    )(scalars, view, view, v_t, subln_g.reshape(HEAD_DIM, 1).astype(F32))
    return out.reshape(batch * seq, WIDTH_B)


OUT_TM = 256


def _layer_norm(y, g, b):
    mu = jnp.mean(y, axis=1, keepdims=True)
    d = y - mu
    var = jnp.mean(d * d, axis=1, keepdims=True)
    return d * lax.rsqrt(var + LN_EPS) * g + b


def _token_order(ref, r, head, scr):
    rows = scr.shape[0]
    for c in range(r):
        lo = c * WIDTH_A + head * HEAD_DIM
        scr[pl.ds(c, rows // r, stride=r), :] = ref[0, :, lo:lo + HEAD_DIM]
    return scr[...]


def _out_proj_kernel(o0, o1, o2, l0, l1, l2, mb_ref, x_ref, w_ref, g_ref, b_ref, wr_ref,
                     x1_ref, x1b_ref, aff_ref, so1, so2, sl1, sl2):
    r1, r2 = DILATIONS[1], DILATIONS[2]
    heads = []
    for h in range(HEADS_A):
        lanes = slice(h * HEAD_DIM, (h + 1) * HEAD_DIM)
        la, lb, lc = l0[:, lanes], _token_order(l1, r1, h, sl1), _token_order(l2, r2, h, sl2)
        mx = jnp.maximum(jnp.maximum(la, lb), lc)
        ea, eb, ec = jnp.exp(la - mx), jnp.exp(lb - mx), jnp.exp(lc - mx)
        num = ea * o0[:, lanes] + eb * _token_order(o1, r1, h, so1) + ec * _token_order(o2, r2, h, so2)
        heads.append((num / (ea + eb + ec)).astype(BF16))
    mix_a = jnp.concatenate(heads, axis=1)
    m = jnp.dot(mix_a, w_ref[:WIDTH_A, :], preferred_element_type=F32)
    m = m + jnp.dot(mb_ref[...], w_ref[WIDTH_A:, :], preferred_element_type=F32)
    x1 = _layer_norm(ALPHA * x_ref[...] + m, g_ref[...], b_ref[...])
    x1_ref[...] = x1
    x1b_ref[...] = x1.astype(BF16)
    logits = jnp.dot(x1, wr_ref[...], preferred_element_type=F32, precision=lax.Precision.HIGHEST)
    e = jnp.exp(logits - jnp.max(logits, axis=1, keepdims=True))
    aff_ref[...] = e / jnp.sum(e, axis=1, keepdims=True)


def _out_proj(o_groups, lse_groups, mix_b, x2d, w_out_bf, ln_g, ln_b, w_router, seq):
    n = x2d.shape[0]
    tm = OUT_TM
    tiles_per_seq = seq // tm
    r1, r2 = DILATIONS[1], DILATIONS[2]
    row_a = pl.BlockSpec((tm, WIDTH_A), lambda i: (i, 0))
    row_d = pl.BlockSpec((tm, D_MODEL), lambda i: (i, 0))
    seq_map = lambda i: (i // tiles_per_seq, i % tiles_per_seq, 0)
    res1 = pl.BlockSpec((1, tm // r1, r1 * WIDTH_A), seq_map)
    res2 = pl.BlockSpec((1, tm // r2, r2 * WIDTH_A), seq_map)
    const = lambda shape: pl.BlockSpec(shape, lambda i: (0, 0))
    return pl.pallas_call(
        _out_proj_kernel,
        grid=(n // tm,),
        in_specs=[row_a, res1, res2, row_a, res1, res2, row_a, row_d, const((D_MODEL, D_MODEL)),
                  const((1, D_MODEL)), const((1, D_MODEL)), const((D_MODEL, N_EXPERTS))],
        out_specs=[row_d, row_d, pl.BlockSpec((tm, N_EXPERTS), lambda i: (i, 0))],
        out_shape=[jax.ShapeDtypeStruct((n, D_MODEL), F32), jax.ShapeDtypeStruct((n, D_MODEL), BF16),
                   jax.ShapeDtypeStruct((n, N_EXPERTS), F32)],
        scratch_shapes=[pltpu.VMEM((tm, HEAD_DIM), F32)] * 4,
        compiler_params=_cparams("parallel"),
        name="out_proj_ln_router",
    )(*o_groups, *lse_groups, mix_b, x2d, w_out_bf, ln_g.reshape(1, D_MODEL), ln_b.reshape(1, D_MODEL), w_router)


ONE_BITS = 0x3F800000


def _threshold_kernel(aff_ref, thr_ref, *, cap):
    bits = pltpu.bitcast(aff_ref[...], jnp.int32)
    e = bits.shape[0]

    def body(_, carry):
        lo, hi = carry
        mid = lo + (hi - lo) // 2
        cnt = jnp.sum((bits >= mid).astype(jnp.int32), axis=1, keepdims=True)
        take = cnt >= cap
        return jnp.where(take, mid, lo), jnp.where(take, hi, mid)

    lo, _ = lax.fori_loop(0, 31, body, (jnp.zeros((e, 1), jnp.int32), jnp.full((e, 1), ONE_BITS + 1, jnp.int32)))
    thr_ref[...] = jnp.broadcast_to(lo, thr_ref.shape)


def _expert_choice(aff_t, cap):
    e, n = aff_t.shape
    thr = pl.pallas_call(
        functools.partial(_threshold_kernel, cap=cap),
        out_shape=jax.ShapeDtypeStruct((e, HEAD_DIM), jnp.int32),
        compiler_params=pltpu.CompilerParams(vmem_limit_bytes=VMEM_LIMIT),
        name="expert_threshold",
    )(aff_t)[:, :1]
    bits = lax.bitcast_convert_type(aff_t, jnp.int32)
    above = bits > thr
    at = bits == thr
    need = cap - jnp.sum(above, axis=1, keepdims=True)
    picked = above | (at & (jnp.cumsum(at, axis=1) <= need))
    rank = jnp.cumsum(picked.astype(jnp.int32), axis=1)
    slots = jnp.arange(1, cap + 1, dtype=jnp.int32)
    idx = jax.vmap(lambda row: jnp.searchsorted(row, slots, side="left"))(rank).astype(jnp.int32)
    slot = jnp.where(picked, rank - 1, -1)
    first = jnp.concatenate([jnp.zeros((e, 1), jnp.int32), rank[:, CMB_TM - 1:-1:CMB_TM]], axis=1)
    return jnp.take_along_axis(aff_t, idx, axis=1), idx, slot, first


CMB_TM = 128
CMB_RB = 64
CMB_WIN = CMB_TM // CMB_RB + 1


def _combine_kernel(blk_ref, slot_ref, x_ref, g_ref, b_ref, *rest):
    ye_refs, o_ref = rest[:-1], rest[-1]
    i = pl.program_id(0)
    slot = slot_ref[...]
    tm = slot.shape[0]
    span = CMB_WIN * CMB_RB
    lane = lax.broadcasted_iota(jnp.int32, (tm, span), 1)
    one, zero = jnp.ones((tm, span), F32), jnp.zeros((tm, span), F32)
    parts = []
    for e in range(N_EXPERTS):
        local = slot[:, e:e + 1] - blk_ref[i * N_EXPERTS + e] * CMB_RB
        parts.append(jnp.where(lane == local, one, zero).astype(BF16))
    onehot = jnp.concatenate(parts, axis=1)
    rows = jnp.concatenate([r[0] for r in ye_refs], axis=0)
    f = jnp.dot(onehot, rows, preferred_element_type=F32)
    o_ref[...] = _layer_norm(ALPHA * x_ref[...] + f, g_ref[...], b_ref[...])


def _combine_ln(ye, slot, first_blk, x1, ln_g, ln_b):
    n = x1.shape[0]
    e, rows, _ = ye.shape
    last = rows // CMB_RB - 1
    row = pl.BlockSpec((CMB_TM, D_MODEL), lambda i, blk: (i, 0))
    const = pl.BlockSpec((1, D_MODEL), lambda i, blk: (0, 0))

    def ye_spec(ei, k):
        return pl.BlockSpec((1, CMB_RB, D_MODEL),
                            lambda i, blk: (ei, jnp.minimum(blk[i * N_EXPERTS + ei] + k, last), 0))

    grid_spec = pltpu.PrefetchScalarGridSpec(
        num_scalar_prefetch=1,
        grid=(n // CMB_TM,),
        in_specs=[pl.BlockSpec((CMB_TM, N_EXPERTS), lambda i, blk: (i, 0)), row, const, const]
        + [ye_spec(ei, k) for ei in range(e) for k in range(CMB_WIN)],
        out_specs=row,
    )
    return pl.pallas_call(
        _combine_kernel,
        grid_spec=grid_spec,
        out_shape=jax.ShapeDtypeStruct((n, D_MODEL), F32),
        compiler_params=_cparams("arbitrary"),
        name="combine_ln",
    )(first_blk, slot, x1, ln_g.reshape(1, D_MODEL), ln_b.reshape(1, D_MODEL), *([ye] * (e * CMB_WIN)))


FFN_TM = 512
FFN_CHUNK = 256
CAST_ROWS = 512


def _cast_kernel(w_ref, o_ref):
    o_ref[...] = w_ref[...].astype(BF16)


def _layer_weights_bf16(w, l):
    _, e, a, b = w.shape
    rows = math.gcd(a, CAST_ROWS)
    return pl.pallas_call(
        _cast_kernel,
        grid=(e, a // rows),
        in_specs=[pl.BlockSpec((None, None, rows, b), lambda ei, i: (l, ei, i, 0))],
        out_specs=pl.BlockSpec((None, rows, b), lambda ei, i: (ei, i, 0)),
        out_shape=jax.ShapeDtypeStruct((e, a, b), BF16),
        compiler_params=_cparams("parallel", "parallel"),
        name="weights_to_bf16",
    )(w)


def _ffn_kernel(x_ref, gate_ref, w1_ref, w3_ref, w2_ref, o_ref):
    x = x_ref[0]
    acc = jnp.zeros((x.shape[0], D_MODEL), F32)
    for c in range(D_FF // FFN_CHUNK):
        cols = slice(c * FFN_CHUNK, (c + 1) * FFN_CHUNK)
        g = jnp.dot(x, w1_ref[0, :, cols], preferred_element_type=F32)
        u = jnp.dot(x, w3_ref[0, :, cols], preferred_element_type=F32)
        h = (g * jax.nn.sigmoid(g) * u).astype(BF16)
        acc = acc + jnp.dot(h, w2_ref[0, cols, :], preferred_element_type=F32)
    o_ref[0] = (acc * gate_ref[0]).astype(BF16)


def _expert_ffn(xe, gates, w1, w3, w2):
    e, rows, _ = xe.shape
    tm = FFN_TM
    return pl.pallas_call(
        _ffn_kernel,
        grid=(e, rows // tm),
        in_specs=[
            pl.BlockSpec((1, tm, D_MODEL), lambda ei, i: (ei, i, 0)),
            pl.BlockSpec((1, tm, 1), lambda ei, i: (ei, i, 0)),
            pl.BlockSpec((1, D_MODEL, D_FF), lambda ei, i: (ei, 0, 0)),
            pl.BlockSpec((1, D_MODEL, D_FF), lambda ei, i: (ei, 0, 0)),
            pl.BlockSpec((1, D_FF, D_MODEL), lambda ei, i: (ei, 0, 0)),
        ],
        out_specs=pl.BlockSpec((1, tm, D_MODEL), lambda ei, i: (ei, i, 0)),
        out_shape=jax.ShapeDtypeStruct((e, rows, D_MODEL), BF16),
        compiler_params=_cparams("arbitrary", "arbitrary"),
        name="expert_ffn",
    )(xe, gates, w1, w3, w2)


def _trunk(x2d, group_rows, batch, seq, w_in, w_out, lambda_q1, lambda_k1, lambda_q2, lambda_k2, subln_g,
           ln1_g, ln1_b, w_router, w_gate, w_up, w_down, ln2_g, ln2_b):
    tables = _rope_tables(seq)
    depth = w_in.shape[0]
    for l in range(depth):
        lambda_init = 0.8 - 0.6 * math.exp(-0.3 * l)
        lam = (jnp.exp(jnp.sum(lambda_q1[l] * lambda_k1[l])) - jnp.exp(jnp.sum(lambda_q2[l] * lambda_k2[l]))
               + lambda_init)
        scalars = jnp.stack([lam, jnp.asarray(1.0 - lambda_init, F32)]).astype(F32)

        *qkv_groups, qkv_b = _in_proj(x2d, _prep_w_in(w_in[l]), tables, batch, seq)
        o_groups, lse_groups = [], []
        for g in range(len(DILATIONS)):
            o, lse = _dilated_group(qkv_groups[g], batch, seq, g)
            o_groups.append(o)
            lse_groups.append(lse)
        o_groups[0] = o_groups[0].reshape(batch * seq, WIDTH_A)
        lse_groups[0] = lse_groups[0].reshape(batch * seq, WIDTH_A)
        mix_b = _diff_attention(qkv_b, scalars, subln_g[l], batch, seq)
        x1, x1b, aff = _out_proj(o_groups, lse_groups, mix_b, x2d, w_out[l].astype(BF16), ln1_g[l], ln1_b[l],
                                 w_router[l], seq)

        xe_parts, gate_parts, slot_parts, blk_parts = [], [], [], []
        start = row_off = 0
        for rows in group_rows:
            cap = CAPACITY_FACTOR * rows // N_EXPERTS
            gates, idx, slot, first = _expert_choice(aff[start:start + rows].T, cap)
            xe_parts.append(idx + start)
            gate_parts.append(gates)
            slot_parts.append(jnp.where(slot >= 0, slot + row_off, -1))
            blk_parts.append((first + row_off) // CMB_RB)
            start += rows
            row_off += cap
        xe = x1b[jnp.concatenate(xe_parts, axis=1)]
        gates = jnp.concatenate(gate_parts, axis=1)
        slot = jnp.concatenate(slot_parts, axis=1).T
        first_blk = jnp.concatenate(blk_parts, axis=1).T.reshape(-1)
        ye = _expert_ffn(xe, gates[..., None], _layer_weights_bf16(w_gate, l), _layer_weights_bf16(w_up, l),
                         _layer_weights_bf16(w_down, l))
        x2d = _combine_ln(ye, slot, first_blk, x1, ln2_g[l], ln2_b[l])
    return x2d


def kernel(x_prompt, x_sample, w_in, w_out, lambda_q1, lambda_k1, lambda_q2, lambda_k2, subln_g, ln1_g, ln1_b,
           w_router, w_gate, w_up, w_down, ln2_g, ln2_b):
    bp, seq, d = x_prompt.shape
    bs = x_sample.shape[0]
    assert x_sample.shape[1] == seq
    x2d = jnp.concatenate([x_prompt.reshape(bp * seq, d), x_sample.reshape(bs * seq, d)], axis=0)
    y = _trunk(x2d, (bp * seq, bs * seq), bp + bs, seq, w_in, w_out, lambda_q1, lambda_k1, lambda_q2, lambda_k2,
               subln_g, ln1_g, ln1_b, w_router, w_gate, w_up, w_down, ln2_g, ln2_b)
    return y[:bp * seq].reshape(bp, seq, d), y[bp * seq:].reshape(bs, seq, d)
```

```python
import functools
import math

import jax
import jax.numpy as jnp
from jax import lax
from jax.experimental import pallas as pl
from jax.experimental.pallas import tpu as pltpu

D_MODEL = 1024
DEPTH = 4
DILATIONS = (1, 4, 16)
RADIUS = 64
HEADS_A = 4
HEAD_DIM = 128
WIDTH_A = HEADS_A * HEAD_DIM
QKV_A = len(DILATIONS) * WIDTH_A
HEADS_B = 4
SUB_DIM_B = 64
WIDTH_B = HEADS_B * HEAD_DIM
IN_COLS = 3 * QKV_A + 3 * WIDTH_B
N_EXPERTS = 16
D_FF = 2816
CAPACITY_FACTOR = 2
ROPE_THETA = 10000.0
LN_EPS = 1e-5
NEG = -1e30
ALPHA = (2 * DEPTH) ** 0.25
LOG2E = 1.4426950408889634
LN2 = 0.6931471805599453

QA_BLK, KA_BLK, VA_BLK = 0, QKV_A // HEAD_DIM, 2 * QKV_A // HEAD_DIM
QB_BLK = 3 * QKV_A // HEAD_DIM
KB_BLK = QB_BLK + WIDTH_B // HEAD_DIM
VB_BLK = KB_BLK + WIDTH_B // HEAD_DIM
N_BLKS = IN_COLS // HEAD_DIM

VMEM_LIMIT = 56 * 1024 * 1024
BF16 = jnp.bfloat16
F32 = jnp.float32


def _cparams(*sem):
    return pltpu.CompilerParams(dimension_semantics=sem, vmem_limit_bytes=VMEM_LIMIT)


IN_TM = 512
IN_CHUNK = 512


def _in_proj_kernel(x_ref, w_ref, tab_ref, a0_ref, a1_ref, a2_ref, b_ref, scr_ref):
    xb = x_ref[...].astype(BF16)
    tm = xb.shape[0]
    group_refs = (a0_ref, a1_ref, a2_ref)
    n_groups = len(DILATIONS)
    for ch in range(IN_COLS // IN_CHUNK):
        c0 = ch * IN_CHUNK
        acc = jnp.dot(xb, w_ref[:, c0:c0 + IN_CHUNK], preferred_element_type=F32)
        if ch < 3 * n_groups:
            sec, g = divmod(ch, n_groups)
            tab = (0, 2, None)[sec]
        else:
            sec, g = ch - 3 * n_groups, None
            tab = (4, 6, None)[sec]
        blocks = []
        for j in range(IN_CHUNK // HEAD_DIM):
            blk = acc[:, j * HEAD_DIM:(j + 1) * HEAD_DIM]
            if tab is not None:
                blk = blk * tab_ref[tab] + pltpu.roll(blk, HEAD_DIM // 2, 1) * tab_ref[tab + 1]
            blocks.append(blk)
        if g is None or g == 0:
            dst = b_ref if g is None else a0_ref
            for j, blk in enumerate(blocks):
                lo = sec * WIDTH_A + j * HEAD_DIM
                dst[:, lo:lo + HEAD_DIM] = blk.astype(BF16)
        else:
            r = DILATIONS[g]
            for j, blk in enumerate(blocks):
                scr_ref[j] = blk
            for c in range(r):
                for j in range(len(blocks)):
                    rows = scr_ref[j, pl.ds(c, tm // r, stride=r), :]
                    lo = c * 3 * WIDTH_A + sec * WIDTH_A + j * HEAD_DIM
                    group_refs[g][0, :, lo:lo + HEAD_DIM] = rows.astype(BF16)


def _in_proj(x2d, w_bf, tables, batch, seq):
    n = x2d.shape[0]
    tm = min(IN_TM, seq)
    tiles_per_seq = seq // tm
    r1, r2 = DILATIONS[1], DILATIONS[2]
    qkv = 3 * WIDTH_A
    seq_map = lambda i: (i // tiles_per_seq, i % tiles_per_seq, 0)
    return pl.pallas_call(
        _in_proj_kernel,
        grid=(n // tm,),
        in_specs=[
            pl.BlockSpec((tm, D_MODEL), lambda i: (i, 0)),
            pl.BlockSpec((D_MODEL, IN_COLS), lambda i: (0, 0), pipeline_mode=pl.Buffered(1)),
            pl.BlockSpec((8, tm, HEAD_DIM), lambda i: (0, i % tiles_per_seq, 0)),
        ],
        out_specs=[
            pl.BlockSpec((tm, qkv), lambda i: (i, 0)),
            pl.BlockSpec((1, tm // r1, r1 * qkv), seq_map),
            pl.BlockSpec((1, tm // r2, r2 * qkv), seq_map),
            pl.BlockSpec((tm, qkv), lambda i: (i, 0)),
        ],
        out_shape=[
            jax.ShapeDtypeStruct((n, qkv), BF16),
            jax.ShapeDtypeStruct((batch, seq // r1, r1 * qkv), BF16),
            jax.ShapeDtypeStruct((batch, seq // r2, r2 * qkv), BF16),
            jax.ShapeDtypeStruct((n, qkv), BF16),
        ],
        scratch_shapes=[pltpu.VMEM((IN_CHUNK // HEAD_DIM, tm, HEAD_DIM), F32)],
        compiler_params=_cparams("parallel"),
        name="in_proj_rope",
    )(x2d, w_bf, tables)


def _rope_tables(seq):
    pos = jnp.arange(seq, dtype=F32)[:, None]

    def cs(d):
        inv = ROPE_THETA ** (-jnp.arange(0, d, 2, dtype=F32) / d)
        ang = pos * inv[None, :]
        return jnp.cos(ang), jnp.sin(ang)

    ca, sa = cs(HEAD_DIM)
    cos_a = jnp.concatenate([ca, ca], axis=1)
    sin_a = jnp.concatenate([-sa, sa], axis=1)
    cb, sb = cs(SUB_DIM_B)
    cos_b = jnp.concatenate([cb, cb, cb, cb], axis=1)
    sin_b = jnp.concatenate([-sb, -sb, sb, sb], axis=1)
    qa = HEAD_DIM ** -0.5 * LOG2E
    qb = SUB_DIM_B ** -0.5 * LOG2E
    return jnp.stack([cos_a * qa, sin_a * qa, cos_a, sin_a, cos_b * qb, sin_b * qb, cos_b, sin_b], axis=0)


def _prep_w_in(w_in):
    half = SUB_DIM_B // 2
    perm = []
    for quarter in range(4):
        t, hi = quarter % 2, quarter // 2
        perm.extend(t * SUB_DIM_B + hi * half + i for i in range(half))
    perm = jnp.asarray(perm, dtype=jnp.int32)
    cols = jnp.arange(IN_COLS, dtype=jnp.int32)
    start, stop = QB_BLK * HEAD_DIM, VB_BLK * HEAD_DIM
    inside = (cols >= start) & (cols < stop)
    rel = cols - start
    permuted = start + (rel // HEAD_DIM) * HEAD_DIM + perm[rel % HEAD_DIM]
    src = jnp.where(inside, permuted, cols)
    return jnp.take(w_in, src, axis=1).astype(BF16)


A_QB = 128
A_WIN = A_QB + 2 * RADIUS
A_UNROLL = 8


def _dilated_kernel(q_ref, k_ref, v_ref, o_ref, lse_ref, *, sub_len, heads):
    nq = sub_len // A_QB
    row = lax.broadcasted_iota(jnp.int32, (A_QB, A_WIN), 0)
    col = lax.broadcasted_iota(jnp.int32, (A_QB, A_WIN), 1)
    rel0 = col - row

    for h in range(heads):
        lanes = slice(h * HEAD_DIM, (h + 1) * HEAD_DIM)

        def body(qi, carry, lanes=lanes):
            q0 = pl.multiple_of(qi * A_QB, A_QB)
            ws = pl.multiple_of(jnp.clip(q0 - RADIUS, 0, sub_len - A_WIN), RADIUS)
            qb = q_ref[0, pl.ds(q0, A_QB), lanes]
            kw = k_ref[0, pl.ds(ws, A_WIN), lanes]
            vw = v_ref[0, pl.ds(ws, A_WIN), lanes]
            s = lax.dot_general(qb, kw, (((1,), (1,)), ((), ())), preferred_element_type=F32)
            rel = rel0 - (q0 - ws)
            s = jnp.where(jnp.abs(rel) <= RADIUS, s, NEG)
            m = jnp.max(s, axis=1, keepdims=True)
            p = jnp.exp2(s - m)
            l = jnp.sum(p, axis=1, keepdims=True)
            o = jnp.dot(p.astype(BF16), vw, preferred_element_type=F32)
            o_ref[0, pl.ds(q0, A_QB), lanes] = o / l
            lse = (m + jnp.log2(l)) * LN2
            lse_ref[0, pl.ds(q0, A_QB), lanes] = jnp.broadcast_to(lse, (A_QB, HEAD_DIM))
            return carry

        lax.fori_loop(0, nq, body, 0, unroll=min(nq, A_UNROLL))


def _dilated_group(qkv, batch, seq, g):
    r = DILATIONS[g]
    sub_len = seq // r
    assert sub_len % A_QB == 0 and sub_len >= A_WIN
    heads = 1 if r == 1 else HEADS_A
    width = heads * HEAD_DIM
    steps = HEADS_A // heads
    per_sec = WIDTH_A // width
    view = qkv.reshape(batch, sub_len, r * 3 * WIDTH_A)

    def in_map(sec):
        return lambda b, c, hh: (b, 0, (c * 3 + sec) * per_sec + hh)

    out_map = lambda b, c, hh: (b, 0, c * steps + hh)
    blk = (1, sub_len, width)
    return pl.pallas_call(
        functools.partial(_dilated_kernel, sub_len=sub_len, heads=heads),
        grid=(batch, r, steps),
        in_specs=[pl.BlockSpec(blk, in_map(0)), pl.BlockSpec(blk, in_map(1)), pl.BlockSpec(blk, in_map(2))],
        out_specs=[pl.BlockSpec(blk, out_map), pl.BlockSpec(blk, out_map)],
        out_shape=[jax.ShapeDtypeStruct((batch, sub_len, r * WIDTH_A), F32)] * 2,
        compiler_params=_cparams("parallel", "parallel", "parallel"),
        name=f"dilated_attn_r{r}",
    )(view, view, view)


B_TQ = 1024
B_TK = 512
B_UNROLL = 4


def _diff_kernel(scal_ref, q_ref, k_ref, vt_ref, g_ref, o_ref, *s_scrs, seq):
    lam = scal_ref[0]
    out_scale = scal_ref[1]
    q = q_ref[0]
    tq = q.shape[0]
    lane = lax.broadcasted_iota(jnp.int32, q.shape, 1)
    first = ((lane // (SUB_DIM_B // 2)) % 2) == 0
    zero = jnp.zeros_like(q)
    nk = seq // B_TK
    qs = (jnp.where(first, q, zero), jnp.where(first, zero, q))

    def scores(j, t, m):
        k0 = pl.multiple_of(j * B_TK, B_TK)
        s = lax.dot_general(k_ref[0, pl.ds(k0, B_TK), :], qs[t], (((1,), (1,)), ((), ())),
                            preferred_element_type=F32)
        s_scrs[t][pl.ds(k0, B_TK), :] = s
        return jnp.maximum(m, jnp.max(s, axis=0, keepdims=True))

    def weigh(j, t, m, l, acc):
        k0 = pl.multiple_of(j * B_TK, B_TK)
        p = jnp.exp2(s_scrs[t][pl.ds(k0, B_TK), :] - m)
        l = l + jnp.sum(p, axis=0, keepdims=True)
        acc = acc + jnp.dot(vt_ref[0, 0, :, pl.ds(k0, B_TK)], p.astype(BF16), preferred_element_type=F32)
        return l, acc

    m_init = jnp.full((1, tq), -jnp.inf, F32)
    la_init = (jnp.zeros((1, tq), F32), jnp.zeros((HEAD_DIM, tq), F32))
    m0 = lax.fori_loop(0, nk, lambda j, m: scores(j, 0, m), m_init, unroll=B_UNROLL)

    def middle(j, carry):
        m1, l0, acc0 = carry
        l0, acc0 = weigh(j, 0, m0, l0, acc0)
        return scores(j, 1, m1), l0, acc0

    m1, l0, acc0 = lax.fori_loop(0, nk, middle, (m_init,) + la_init, unroll=B_UNROLL)
    l1, acc1 = lax.fori_loop(0, nk, lambda j, c: weigh(j, 1, m1, *c), la_init, unroll=B_UNROLL)
    o = acc0 / l0 - lam * (acc1 / l1)
    o = o * lax.rsqrt(jnp.mean(o * o, axis=0, keepdims=True) + LN_EPS) * g_ref[...]
    o_ref[0] = (o * out_scale).T.astype(BF16)


def _diff_attention(qkv_b, scalars, subln_g, batch, seq):
    view = qkv_b.reshape(batch, seq, 3 * WIDTH_B)
    v_t = view[:, :, 2 * WIDTH_B:].reshape(batch, seq, HEADS_B, HEAD_DIM).transpose(0, 2, 3, 1)
    tq = min(B_TQ, seq)
    out = pl.pallas_call(
        functools.partial(_diff_kernel, seq=seq),
        grid=(batch, HEADS_B, seq // tq),
        in_specs=[
            pl.BlockSpec(memory_space=pltpu.SMEM),
            pl.BlockSpec((1, tq, HEAD_DIM), lambda b, h, i: (b, i, h)),
            pl.BlockSpec((1, seq, HEAD_DIM), lambda b, h, i: (b, 0, HEADS_B + h)),
            pl.BlockSpec((1, 1, HEAD_DIM, seq), lambda b, h, i: (b, h, 0, 0)),
            pl.BlockSpec((HEAD_DIM, 1), lambda b, h, i: (0, 0)),
        ],
        out_specs=pl.BlockSpec((1, tq, HEAD_DIM), lambda b, h, i: (b, i, h)),
        out_shape=jax.ShapeDtypeStruct((batch, seq, WIDTH_B), BF16),
        scratch_shapes=[pltpu.VMEM((seq, tq), F32)] * 2,
        compiler_params=_cparams("parallel", "parallel", "arbitrary"),
        name="diff_attn",
    )(scalars, view, view, v_t, subln_g.reshape(HEAD_DIM, 1).astype(F32))
    return out.reshape(batch * seq, WIDTH_B)


OUT_TM = 256


def _layer_norm(y, g, b):
    mu = jnp.mean(y, axis=1, keepdims=True)
    d = y - mu
    var = jnp.mean(d * d, axis=1, keepdims=True)
    return d * lax.rsqrt(var + LN_EPS) * g + b


def _token_order(ref, r, head, scr):
    rows = scr.shape[0]
    for c in range(r):
        lo = c * WIDTH_A + head * HEAD_DIM
        scr[pl.ds(c, rows // r, stride=r), :] = ref[0, :, lo:lo + HEAD_DIM]
    return scr[...]


def _out_proj_kernel(o0, o1, o2, l0, l1, l2, mb_ref, x_ref, w_ref, g_ref, b_ref, wr_ref,
                     x1_ref, x1b_ref, aff_ref, so1, so2, sl1, sl2):
    r1, r2 = DILATIONS[1], DILATIONS[2]
    heads = []
    for h in range(HEADS_A):
        lanes = slice(h * HEAD_DIM, (h + 1) * HEAD_DIM)
        la, lb, lc = l0[:, lanes], _token_order(l1, r1, h, sl1), _token_order(l2, r2, h, sl2)
        mx = jnp.maximum(jnp.maximum(la, lb), lc)
        ea, eb, ec = jnp.exp(la - mx), jnp.exp(lb - mx), jnp.exp(lc - mx)
        num = ea * o0[:, lanes] + eb * _token_order(o1, r1, h, so1) + ec * _token_order(o2, r2, h, so2)
        heads.append((num / (ea + eb + ec)).astype(BF16))
    mix_a = jnp.concatenate(heads, axis=1)
    m = jnp.dot(mix_a, w_ref[:WIDTH_A, :], preferred_element_type=F32)
    m = m + jnp.dot(mb_ref[...], w_ref[WIDTH_A:, :], preferred_element_type=F32)
    x1 = _layer_norm(ALPHA * x_ref[...] + m, g_ref[...], b_ref[...])
    x1_ref[...] = x1
    x1b_ref[...] = x1.astype(BF16)
    logits = jnp.dot(x1, wr_ref[...], preferred_element_type=F32, precision=lax.Precision.HIGHEST)
    e = jnp.exp(logits - jnp.max(logits, axis=1, keepdims=True))
    aff_ref[...] = e / jnp.sum(e, axis=1, keepdims=True)


def _out_proj(o_groups, lse_groups, mix_b, x2d, w_out_bf, ln_g, ln_b, w_router, seq):
    n = x2d.shape[0]
    tm = OUT_TM
    tiles_per_seq = seq // tm
    r1, r2 = DILATIONS[1], DILATIONS[2]
    row_a = pl.BlockSpec((tm, WIDTH_A), lambda i: (i, 0))
    row_d = pl.BlockSpec((tm, D_MODEL), lambda i: (i, 0))
    seq_map = lambda i: (i // tiles_per_seq, i % tiles_per_seq, 0)
    res1 = pl.BlockSpec((1, tm // r1, r1 * WIDTH_A), seq_map)
    res2 = pl.BlockSpec((1, tm // r2, r2 * WIDTH_A), seq_map)
    const = lambda shape: pl.BlockSpec(shape, lambda i: (0, 0))
    return pl.pallas_call(
        _out_proj_kernel,
        grid=(n // tm,),
        in_specs=[row_a, res1, res2, row_a, res1, res2, row_a, row_d, const((D_MODEL, D_MODEL)),
                  const((1, D_MODEL)), const((1, D_MODEL)), const((D_MODEL, N_EXPERTS))],
        out_specs=[row_d, row_d, pl.BlockSpec((tm, N_EXPERTS), lambda i: (i, 0))],
        out_shape=[jax.ShapeDtypeStruct((n, D_MODEL), F32), jax.ShapeDtypeStruct((n, D_MODEL), BF16),
                   jax.ShapeDtypeStruct((n, N_EXPERTS), F32)],
        scratch_shapes=[pltpu.VMEM((tm, HEAD_DIM), F32)] * 4,
        compiler_params=_cparams("parallel"),
        name="out_proj_ln_router",
    )(*o_groups, *lse_groups, mix_b, x2d, w_out_bf, ln_g.reshape(1, D_MODEL), ln_b.reshape(1, D_MODEL), w_router)


ONE_BITS = 0x3F800000


def _threshold_kernel(aff_ref, thr_ref, *, cap):
    bits = pltpu.bitcast(aff_ref[...], jnp.int32)
    e = bits.shape[0]

    def body(_, carry):
        lo, hi = carry
        mid = lo + (hi - lo) // 2
        cnt = jnp.sum((bits >= mid).astype(jnp.int32), axis=1, keepdims=True)
        take = cnt >= cap
        return jnp.where(take, mid, lo), jnp.where(take, hi, mid)

    lo, _ = lax.fori_loop(0, 31, body, (jnp.zeros((e, 1), jnp.int32), jnp.full((e, 1), ONE_BITS + 1, jnp.int32)))
    thr_ref[...] = jnp.broadcast_to(lo, thr_ref.shape)


def _expert_choice(aff_t, cap):
    e, n = aff_t.shape
    thr = pl.pallas_call(
        functools.partial(_threshold_kernel, cap=cap),
        out_shape=jax.ShapeDtypeStruct((e, HEAD_DIM), jnp.int32),
        compiler_params=pltpu.CompilerParams(vmem_limit_bytes=VMEM_LIMIT),
        name="expert_threshold",
    )(aff_t)[:, :1]
    bits = lax.bitcast_convert_type(aff_t, jnp.int32)
    above = bits > thr
    at = bits == thr
    need = cap - jnp.sum(above, axis=1, keepdims=True)
    picked = above | (at & (jnp.cumsum(at, axis=1) <= need))
    rank = jnp.cumsum(picked.astype(jnp.int32), axis=1)
    rows_j = jnp.arange(cap, dtype=jnp.int32)
    ends = rank[:, CMB_TM - 1::CMB_TM]
    tile = jnp.sum((ends[:, :, None] <= rows_j[None, None, :]).astype(jnp.int32), axis=1)
    tile_ranks = jnp.take_along_axis(rank.reshape(e, n // CMB_TM, CMB_TM), tile[:, :, None], axis=1)
    idx = tile * CMB_TM + jnp.sum((tile_ranks <= rows_j[None, :, None]).astype(jnp.int32), axis=2)
    slot = jnp.where(picked, rank - 1, -1)
    first = jnp.concatenate([jnp.zeros((e, 1), jnp.int32), rank[:, CMB_TM - 1:-1:CMB_TM]], axis=1)
    return jnp.take_along_axis(aff_t, idx, axis=1), idx, slot, first


CMB_TM = 128
CMB_RB = 64
CMB_WIN = CMB_TM // CMB_RB + 1


def _combine_kernel(blk_ref, slot_ref, x_ref, g_ref, b_ref, *rest):
    ye_refs, o_ref = rest[:-1], rest[-1]
    i = pl.program_id(0)
    slot = slot_ref[...]
    tm = slot.shape[0]
    span = CMB_WIN * CMB_RB
    lane = lax.broadcasted_iota(jnp.int32, (tm, span), 1)
    one, zero = jnp.ones((tm, span), F32), jnp.zeros((tm, span), F32)
    parts = []
    for e in range(N_EXPERTS):
        local = slot[:, e:e + 1] - blk_ref[i * N_EXPERTS + e] * CMB_RB
        parts.append(jnp.where(lane == local, one, zero).astype(BF16))
    onehot = jnp.concatenate(parts, axis=1)
    rows = jnp.concatenate([r[0] for r in ye_refs], axis=0)
    f = jnp.dot(onehot, rows, preferred_element_type=F32)
    o_ref[...] = _layer_norm(ALPHA * x_ref[...] + f, g_ref[...], b_ref[...])


def _combine_ln(ye, slot, first_blk, x1, ln_g, ln_b):
    n = x1.shape[0]
    e, rows, _ = ye.shape
    last = rows // CMB_RB - 1
    row = pl.BlockSpec((CMB_TM, D_MODEL), lambda i, blk: (i, 0))
    const = pl.BlockSpec((1, D_MODEL), lambda i, blk: (0, 0))

    def ye_spec(ei, k):
        return pl.BlockSpec((1, CMB_RB, D_MODEL),
                            lambda i, blk: (ei, jnp.minimum(blk[i * N_EXPERTS + ei] + k, last), 0))

    grid_spec = pltpu.PrefetchScalarGridSpec(
        num_scalar_prefetch=1,
        grid=(n // CMB_TM,),
        in_specs=[pl.BlockSpec((CMB_TM, N_EXPERTS), lambda i, blk: (i, 0)), row, const, const]
        + [ye_spec(ei, k) for ei in range(e) for k in range(CMB_WIN)],
        out_specs=row,
    )
    return pl.pallas_call(
        _combine_kernel,
        grid_spec=grid_spec,
        out_shape=jax.ShapeDtypeStruct((n, D_MODEL), F32),
        compiler_params=_cparams("arbitrary"),
        name="combine_ln",
    )(first_blk, slot, x1, ln_g.reshape(1, D_MODEL), ln_b.reshape(1, D_MODEL), *([ye] * (e * CMB_WIN)))


FFN_TM = 512
FFN_CHUNK = 256
CAST_ROWS = 512


def _cast_kernel(w_ref, o_ref):
    o_ref[...] = w_ref[...].astype(BF16)


def _layer_weights_bf16(w, l):
    _, e, a, b = w.shape
    rows = math.gcd(a, CAST_ROWS)
    return pl.pallas_call(
        _cast_kernel,
        grid=(e, a // rows),
        in_specs=[pl.BlockSpec((None, None, rows, b), lambda ei, i: (l, ei, i, 0))],
        out_specs=pl.BlockSpec((None, rows, b), lambda ei, i: (ei, i, 0)),
        out_shape=jax.ShapeDtypeStruct((e, a, b), BF16),
        compiler_params=_cparams("parallel", "parallel"),
        name="weights_to_bf16",
    )(w)


def _ffn_kernel(x_ref, gate_ref, w1_ref, w3_ref, w2_ref, o_ref):
    x = x_ref[0]
    acc = jnp.zeros((x.shape[0], D_MODEL), F32)
    for c in range(D_FF // FFN_CHUNK):
        cols = slice(c * FFN_CHUNK, (c + 1) * FFN_CHUNK)
        g = jnp.dot(x, w1_ref[0, :, cols], preferred_element_type=F32)
        u = jnp.dot(x, w3_ref[0, :, cols], preferred_element_type=F32)
        h = (g * jax.nn.sigmoid(g) * u).astype(BF16)
        acc = acc + jnp.dot(h, w2_ref[0, cols, :], preferred_element_type=F32)
    o_ref[0] = (acc * gate_ref[0]).astype(BF16)


def _expert_ffn(xe, gates, w1, w3, w2):
    e, rows, _ = xe.shape
    tm = FFN_TM
    return pl.pallas_call(
        _ffn_kernel,
        grid=(e, rows // tm),
        in_specs=[
            pl.BlockSpec((1, tm, D_MODEL), lambda ei, i: (ei, i, 0)),
            pl.BlockSpec((1, tm, 1), lambda ei, i: (ei, i, 0)),
            pl.BlockSpec((1, D_MODEL, D_FF), lambda ei, i: (ei, 0, 0)),
            pl.BlockSpec((1, D_MODEL, D_FF), lambda ei, i: (ei, 0, 0)),
            pl.BlockSpec((1, D_FF, D_MODEL), lambda ei, i: (ei, 0, 0)),
        ],
        out_specs=pl.BlockSpec((1, tm, D_MODEL), lambda ei, i: (ei, i, 0)),
        out_shape=jax.ShapeDtypeStruct((e, rows, D_MODEL), BF16),
        compiler_params=_cparams("arbitrary", "arbitrary"),
        name="expert_ffn",
    )(xe, gates, w1, w3, w2)


def _trunk(x2d, group_rows, batch, seq, w_in, w_out, lambda_q1, lambda_k1, lambda_q2, lambda_k2, subln_g,
           ln1_g, ln1_b, w_router, w_gate, w_up, w_down, ln2_g, ln2_b):
    tables = _rope_tables(seq)
    depth = w_in.shape[0]
    for l in range(depth):
        lambda_init = 0.8 - 0.6 * math.exp(-0.3 * l)
        lam = (jnp.exp(jnp.sum(lambda_q1[l] * lambda_k1[l])) - jnp.exp(jnp.sum(lambda_q2[l] * lambda_k2[l]))
               + lambda_init)
        scalars = jnp.stack([lam, jnp.asarray(1.0 - lambda_init, F32)]).astype(F32)

        *qkv_groups, qkv_b = _in_proj(x2d, _prep_w_in(w_in[l]), tables, batch, seq)
        o_groups, lse_groups = [], []
        for g in range(len(DILATIONS)):
            o, lse = _dilated_group(qkv_groups[g], batch, seq, g)
            o_groups.append(o)
            lse_groups.append(lse)
        o_groups[0] = o_groups[0].reshape(batch * seq, WIDTH_A)
        lse_groups[0] = lse_groups[0].reshape(batch * seq, WIDTH_A)
        mix_b = _diff_attention(qkv_b, scalars, subln_g[l], batch, seq)
        x1, x1b, aff = _out_proj(o_groups, lse_groups, mix_b, x2d, w_out[l].astype(BF16), ln1_g[l], ln1_b[l],
                                 w_router[l], seq)

        xe_parts, gate_parts, slot_parts, blk_parts = [], [], [], []
        start = row_off = 0
        for rows in group_rows:
            cap = CAPACITY_FACTOR * rows // N_EXPERTS
            gates, idx, slot, first = _expert_choice(aff[start:start + rows].T, cap)
            xe_parts.append(idx + start)
            gate_parts.append(gates)
            slot_parts.append(jnp.where(slot >= 0, slot + row_off, -1))
            blk_parts.append((first + row_off) // CMB_RB)
            start += rows
            row_off += cap
        xe = x1b[jnp.concatenate(xe_parts, axis=1)]
        gates = jnp.concatenate(gate_parts, axis=1)
        slot = jnp.concatenate(slot_parts, axis=1).T
        first_blk = jnp.concatenate(blk_parts, axis=1).T.reshape(-1)
        ye = _expert_ffn(xe, gates[..., None], _layer_weights_bf16(w_gate, l), _layer_weights_bf16(w_up, l),
                         _layer_weights_bf16(w_down, l))
        x2d = _combine_ln(ye, slot, first_blk, x1, ln2_g[l], ln2_b[l])
    return x2d


def kernel(x_prompt, x_sample, w_in, w_out, lambda_q1, lambda_k1, lambda_q2, lambda_k2, subln_g, ln1_g, ln1_b,
           w_router, w_gate, w_up, w_down, ln2_g, ln2_b):
    bp, seq, d = x_prompt.shape
    bs = x_sample.shape[0]
    assert x_sample.shape[1] == seq
    x2d = jnp.concatenate([x_prompt.reshape(bp * seq, d), x_sample.reshape(bs * seq, d)], axis=0)
    y = _trunk(x2d, (bp * seq, bs * seq), bp + bs, seq, w_in, w_out, lambda_q1, lambda_k1, lambda_q2, lambda_k2,
               subln_g, ln1_g, ln1_b, w_router, w_gate, w_up, w_down, ln2_g, ln2_b)
    return y[:bp * seq].reshape(bp, seq, d), y[bp * seq:].reshape(bs, seq, d)
```
